```python
import jax, jax.numpy as jnp
from jax import lax
import numpy as np

D_MODEL = 1024
BATCH = 2
SEQ = 16384
DEPTH = 2

N_MIXERS = 2
N_NSA_LAYERS = (DEPTH + 1) // 2
N_HG_LAYERS = DEPTH // 2
EPS = 1e-6
ROPE_THETA = 10000.0

NSA_HEADS = 16
NSA_KV_GROUPS = 4
NSA_HPG = NSA_HEADS // NSA_KV_GROUPS
NSA_HEAD_DIM = D_MODEL // NSA_HEADS
NSA_Q_W = NSA_HEADS * NSA_HEAD_DIM
NSA_KV_W = NSA_KV_GROUPS * NSA_HEAD_DIM
NSA_IN = 2 * NSA_Q_W + 6 * NSA_KV_W + 3 * NSA_HEADS
CMP_BLOCK = 32
CMP_STRIDE = 16
CMP_HIDDEN = 4 * NSA_HEAD_DIM
SEL_BLOCK = 64
SEL_TOPK = 16
WINDOW = 512
NSA_QBLOCK = 128

HG_EXPAND = 128
HG_HEADS = D_MODEL // HG_EXPAND
HG_KDIM = HG_EXPAND
HG_VDIM = D_MODEL // HG_HEADS
HG_K_W = HG_HEADS * HG_KDIM
HG_V_W = HG_HEADS * HG_VDIM
HG_IN = 2 * HG_K_W + 2 * HG_V_W
HG_CHUNK = 64

kernel_name = "nsa_hgrn2_interleaved_hybrid"


def rms_norm(x, g):
    xf = x.astype(jnp.float32)
    y = xf * lax.rsqrt(jnp.mean(xf * xf, axis=-1, keepdims=True) + EPS)
    return (y * g.astype(jnp.float32)).astype(x.dtype)


def rope(x, pos):
    half = x.shape[-1] // 2
    inv = ROPE_THETA ** (-jnp.arange(half, dtype=jnp.float32) / half)
    ang = pos.astype(jnp.float32)[:, None] * inv[None, :]
    cos = jnp.cos(ang)[:, None, :]
    sin = jnp.sin(ang)[:, None, :]
    xf = x.astype(jnp.float32)
    x1, x2 = xf[..., :half], xf[..., half:]
    return jnp.concatenate([x1 * cos - x2 * sin, x2 * cos + x1 * sin], axis=-1).astype(x.dtype)


def masked_softmax(s, mask):
    p = jax.nn.softmax(jnp.where(mask, s, -1e30), axis=-1)
    return jnp.where(mask, p, 0.0)


def split_cols(a, sizes):
    offs = np.cumsum(sizes)[:-1].tolist()
    return jnp.split(a, offs, axis=-1)


def compress(kv, pe, w1, w2):
    B, S, G, hd = kv.shape
    n_cmp = (S - CMP_BLOCK) // CMP_STRIDE + 1
    idx = jnp.arange(n_cmp)[:, None] * CMP_STRIDE + jnp.arange(CMP_BLOCK)[None, :]
    blocks = kv[:, idx] + pe[:, None, :]
    flat = jnp.moveaxis(blocks, 3, 2).reshape(B, n_cmp, G, CMP_BLOCK * hd)
    return jax.nn.silu(flat @ w1) @ w2


def nsa_mixer(h, w_in, pe_k, pe_v, wk1, wk2, wv1, wv2, w_out):
    B, S, _ = h.shape
    H, G, R, hd = NSA_HEADS, NSA_KV_GROUPS, NSA_HPG, NSA_HEAD_DIM
    QB = NSA_QBLOCK
    scale = hd ** -0.5
    pos = jnp.arange(S)
    proj = h @ w_in
    q, kc, vc, ks, vs, kw, vw, gl, z = split_cols(proj, [NSA_Q_W] + [NSA_KV_W] * 6 + [3 * H, NSA_Q_W])
    q = rope(q.reshape(B, S, H, hd), pos).reshape(B, S, G, R, hd)
    gates = jax.nn.sigmoid(gl.astype(jnp.float32)).reshape(B, S, G, R, 3)
    kv = lambda a: a.reshape(B, S, G, hd)

    v_cmp = compress(kv(vc), pe_v, wv1, wv2)
    k_raw = compress(kv(kc), pe_k, wk1, wk2)
    n_cmp = k_raw.shape[1]
    cmp_start = jnp.arange(n_cmp) * CMP_STRIDE
    cmp_end = cmp_start + CMP_BLOCK - 1
    k_cmp = rope(k_raw, cmp_end)

    n_sel = S // SEL_BLOCK
    n_top = min(SEL_TOPK, n_sel)
    ks_blk = rope(kv(ks), pos).reshape(B, n_sel, SEL_BLOCK, G, hd).transpose(0, 3, 1, 2, 4)
    vs_blk = kv(vs).reshape(B, n_sel, SEL_BLOCK, G, hd).transpose(0, 3, 1, 2, 4)
    sel_start = jnp.arange(n_sel) * SEL_BLOCK
    overlap = ((cmp_start[:, None] < sel_start[None, :] + SEL_BLOCK)
               & (cmp_start[:, None] + CMP_BLOCK > sel_start[None, :])).astype(jnp.float32)
    bidx = jnp.arange(B)[:, None, None, None]
    gidx = jnp.arange(G)[None, None, :, None]
    sel_ids = jnp.arange(n_sel)

    pad = ((0, 0), (WINDOW, 0), (0, 0), (0, 0))
    kw_p = jnp.pad(rope(kv(kw), pos), pad)
    vw_p = jnp.pad(kv(vw), pad)

    def block_fn(s0):
        t = s0 + jnp.arange(QB)
        qb = lax.dynamic_slice_in_dim(q, s0, QB, axis=1)
        gb = lax.dynamic_slice_in_dim(gates, s0, QB, axis=1)
        sc = jnp.einsum('bqgrd,bngd->bqgrn', qb, k_cmp).astype(jnp.float32) * scale
        m_c = (cmp_end[None, :] <= t[:, None])[None, :, None, None, :]
        p_c = masked_softmax(sc, m_c)
        o_c = jnp.einsum('bqgrn,bngd->bqgrd', p_c.astype(v_cmp.dtype), v_cmp)
        imp = jnp.einsum('bqgrn,nj->bqgj', p_c, overlap)
        cur = t // SEL_BLOCK
        valid = sel_ids[None, :] <= cur[:, None]
        forced = (sel_ids[None, :] == 0) | (sel_ids[None, :] == cur[:, None]) | (sel_ids[None, :] == cur[:, None] - 1)
        score = jnp.where(forced[None, :, None, :], jnp.inf,
                          jnp.where(valid[None, :, None, :], imp, -jnp.inf))
        _, idx = lax.top_k(score, n_top)
        k_sel = ks_blk[bidx, gidx, idx].reshape(B, QB, G, n_top * SEL_BLOCK, hd)
        v_sel = vs_blk[bidx, gidx, idx].reshape(B, QB, G, n_top * SEL_BLOCK, hd)
        pos_sel = (idx[..., None] * SEL_BLOCK + jnp.arange(SEL_BLOCK)).reshape(B, QB, G, 1, n_top * SEL_BLOCK)
        m_s = pos_sel <= t[None, :, None, None, None]
        ss = jnp.einsum('bqgrd,bqgmd->bqgrm', qb, k_sel).astype(jnp.float32) * scale
        p_s = masked_softmax(ss, m_s)
        o_s = jnp.einsum('bqgrm,bqgmd->bqgrd', p_s.astype(v_sel.dtype), v_sel)
        kwb = lax.dynamic_slice_in_dim(kw_p, s0, WINDOW + QB, axis=1)
        vwb = lax.dynamic_slice_in_dim(vw_p, s0, WINDOW + QB, axis=1)
        pos_w = s0 - WINDOW + jnp.arange(WINDOW + QB)
        dlt = t[:, None] - pos_w[None, :]
        m_w = ((pos_w[None, :] >= 0) & (dlt >= 0) & (dlt < WINDOW))[None, :, None, None, :]
        sw = jnp.einsum('bqgrd,bkgd->bqgrk', qb, kwb).astype(jnp.float32) * scale
        p_w = masked_softmax(sw, m_w)
        o_w = jnp.einsum('bqgrk,bkgd->bqgrd', p_w.astype(vwb.dtype), vwb)
        o = gb[..., 0:1] * o_c + gb[..., 1:2] * o_s + gb[..., 2:3] * o_w
        return o.astype(h.dtype)

    o = lax.map(block_fn, jnp.arange(S // QB) * QB)
    o = o.transpose(1, 0, 2, 3, 4, 5).reshape(B, S, NSA_Q_W)
    return (o * jax.nn.silu(z)) @ w_out


def hgrn2_mixer(h, w_in, lb, g_norm, w_out):
    B, S, _ = h.shape
    H, dk, dv, C = HG_HEADS, HG_KDIM, HG_VDIM, HG_CHUNK
    NC = S // C
    proj = h @ w_in
    q_raw, f_raw, i_in, z = split_cols(proj, [HG_K_W, HG_K_W, HG_V_W, HG_V_W])
    q = jax.nn.silu(q_raw.astype(jnp.float32)).reshape(B, S, H, dk)
    f = lb[None, None, :] + (1.0 - lb[None, None, :]) * jax.nn.sigmoid(f_raw.astype(jnp.float32))
    k = (1.0 - f).reshape(B, S, H, dk)
    logf = jnp.log(f).reshape(B, S, H, dk)
    v = i_in.astype(jnp.float32).reshape(B, S, H, dv)
    to_chunks = lambda a: a.reshape(B, NC, C, H, a.shape[-1]).transpose(1, 0, 3, 2, 4)
    causal = jnp.tril(jnp.ones((C, C), dtype=bool))

    def step(state, inp):
        qc, kc, vc, lfc = inp
        b = jnp.cumsum(lfc, axis=2)
        o_inter = jnp.einsum('bhtk,bhkv->bhtv', qc * jnp.exp(b), state)
        diff = b[:, :, :, None, :] - b[:, :, None, :, :]
        decay = jnp.exp(jnp.where(causal[:, :, None], diff, -jnp.inf))
        att = jnp.einsum('bhtk,bhsk,bhtsk->bhts', qc, kc, decay)
        o = o_inter + jnp.einsum('bhts,bhsv->bhtv', att, vc)
        b_last = b[:, :, -1:, :]
        new_state = jnp.exp(b_last[:, :, 0, :])[..., None] * state + jnp.einsum(
            'bhsk,bhsv->bhkv', kc * jnp.exp(b_last - b), vc)
        return new_state, o

    state0 = jnp.zeros((B, H, dk, dv), jnp.float32)
    _, o = lax.scan(step, state0, (to_chunks(q), to_chunks(k), to_chunks(v), to_chunks(logf)))
    o = o.transpose(1, 0, 3, 2, 4).reshape(B, S, H, dv)
    o = o * lax.rsqrt(jnp.mean(o * o, axis=-1, keepdims=True) + EPS)
    o = (o.reshape(B, S, HG_V_W) * g_norm.astype(jnp.float32)).astype(h.dtype)
    return (o * jax.nn.silu(z)) @ w_out


def setup_inputs(seed: int = 0) -> dict:
    key = jax.random.key(seed)
    ks = jax.random.split(key, 16)
    nrm = lambda k, shape: jax.random.normal(k, shape, jnp.float32)
    w = lambda k, shape, fan_in: nrm(k, shape) * fan_in ** -0.5
    hd = NSA_HEAD_DIM
    return {
        "x": nrm(ks[0], (BATCH, SEQ, D_MODEL)),
        "norm_w": 1.0 + 0.01 * nrm(ks[1], (DEPTH, D_MODEL)),
        "nsa_w_in": w(ks[2], (N_NSA_LAYERS, D_MODEL, NSA_IN), D_MODEL),
        "nsa_pe_k": 0.02 * nrm(ks[3], (N_NSA_LAYERS, CMP_BLOCK, hd)),
        "nsa_pe_v": 0.02 * nrm(ks[4], (N_NSA_LAYERS, CMP_BLOCK, hd)),
        "nsa_wk1": w(ks[5], (N_NSA_LAYERS, CMP_BLOCK * hd, CMP_HIDDEN), CMP_BLOCK * hd),
        "nsa_wk2": w(ks[6], (N_NSA_LAYERS, CMP_HIDDEN, hd), CMP_HIDDEN),
        "nsa_wv1": w(ks[7], (N_NSA_LAYERS, CMP_BLOCK * hd, CMP_HIDDEN), CMP_BLOCK * hd),
        "nsa_wv2": w(ks[8], (N_NSA_LAYERS, CMP_HIDDEN, hd), CMP_HIDDEN),
        "nsa_w_out": w(ks[9], (N_NSA_LAYERS, NSA_Q_W, D_MODEL), NSA_Q_W),
        "hg_w_in": w(ks[10], (N_HG_LAYERS, D_MODEL, HG_IN), D_MODEL),
        "hg_lb_logits": 0.5 * nrm(ks[11], (DEPTH, HG_K_W)),
        "hg_norm": 1.0 + 0.01 * nrm(ks[12], (N_HG_LAYERS, HG_V_W)),
        "hg_w_out": w(ks[13], (N_HG_LAYERS, HG_V_W, D_MODEL), HG_V_W),
        "final_norm": 1.0 + 0.01 * nrm(ks[14], (D_MODEL,)),
    }


def reference(x, norm_w, nsa_w_in, nsa_pe_k, nsa_pe_v, nsa_wk1, nsa_wk2, nsa_wv1, nsa_wv2,
              nsa_w_out, hg_w_in, hg_lb_logits, hg_norm, hg_w_out, final_norm):
    p = jax.nn.softmax(hg_lb_logits.astype(jnp.float32), axis=0)
    lower_bounds = jnp.cumsum(p, axis=0) - p[0]
    for i in range(DEPTH):
        h = rms_norm(x, norm_w[i])
        j = i // N_MIXERS
        if i % N_MIXERS == 0:
            y = nsa_mixer(h, nsa_w_in[j], nsa_pe_k[j], nsa_pe_v[j], nsa_wk1[j], nsa_wk2[j],
                          nsa_wv1[j], nsa_wv2[j], nsa_w_out[j])
        else:
            y = hgrn2_mixer(h, hg_w_in[j], lower_bounds[i], hg_norm[j], hg_w_out[j])
        x = x + y.astype(x.dtype)
    return rms_norm(x, final_norm)
```

```python
import functools

import jax
import jax.numpy as jnp
import numpy as np
from jax import lax
from jax.experimental import pallas as pl
from jax.experimental.pallas import tpu as pltpu

EPS = 1e-6
ROPE_THETA = 10000.0

NSA_HEADS = 16
NSA_GROUPS = 4
NSA_HPG = NSA_HEADS // NSA_GROUPS
HEAD_DIM = 64
CMP_BLOCK = 32
CMP_STRIDE = 16
SEL_BLOCK = 64
SEL_TOPK = 16
WINDOW = 512

HG_HEAD_DIM = 128
HG_CHUNK = 64

NEG = -1e30
LANES = 128
SEL_GROUP = 64
VMEM_LIMIT = 52 * 1024 * 1024

F32 = jnp.float32
BF16 = jnp.bfloat16


def _cparams(*sem):
    return pltpu.CompilerParams(dimension_semantics=sem, vmem_limit_bytes=VMEM_LIMIT)


def _sigmoid(x):
    return 1.0 / (1.0 + jnp.exp(-x))


def _dot(a, b):
    return jnp.dot(a, b, preferred_element_type=F32)


def _dot_nt(a, b):
    return lax.dot_general(a, b, (((1,), (1,)), ((), ())), preferred_element_type=F32)


def _dot_tn(a, b):
    return lax.dot_general(a, b, (((0,), (0,)), ((), ())), preferred_element_type=F32)


def _split_bf16(x):
    hi = x.astype(BF16)
    lo = (x - hi.astype(F32)).astype(BF16)
    return hi, lo


def _rope_tables(pos, width):
    half = HEAD_DIM // 2
    inv = ROPE_THETA ** (-jnp.arange(half, dtype=F32) / half)
    ang = pos.astype(F32)[:, None] * inv[None, :]
    cos = jnp.concatenate([jnp.cos(ang), jnp.cos(ang)], axis=-1)
    sin = jnp.concatenate([-jnp.sin(ang), jnp.sin(ang)], axis=-1)
    reps = width // HEAD_DIM
    return jnp.tile(cos, (1, reps)), jnp.tile(sin, (1, reps))


def _rope128(x, cos, sin):
    lane = lax.broadcasted_iota(jnp.int32, x.shape, 1)
    first_half = (lane % HEAD_DIM) < (HEAD_DIM // 2)
    partner = jnp.where(first_half, pltpu.roll(x, LANES - HEAD_DIM // 2, 1),
                        pltpu.roll(x, HEAD_DIM // 2, 1))
    return x * cos + partner * sin


def _nsa_proj_kernel(x_ref, g_ref, wq_ref, wkv_ref, wg_ref, wz_ref, cos_ref, sin_ref,
                     q_ref, kvc_ref, ksw_ref, vsw_ref, gate_ref, sz_ref):
    x = x_ref[...]
    h = x * lax.rsqrt(jnp.mean(x * x, axis=-1, keepdims=True) + EPS) * g_ref[...]
    hb = h.astype(BF16)
    cos = cos_ref[...]
    sin = sin_ref[...]
    scale = HEAD_DIM ** -0.5
    q = _dot(hb, wq_ref[...])
    for c in range(q.shape[1] // LANES):
        sl = slice(c * LANES, (c + 1) * LANES)
        q_ref[:, sl] = (_rope128(q[:, sl], cos, sin) * scale).astype(BF16)
    kv = _dot(hb, wkv_ref[...])
    kvc_ref[...] = kv[:, 0:512]
    for c, src in enumerate((2, 4)):
        for half in range(2):
            sl = slice(src * 256 + half * LANES, src * 256 + (half + 1) * LANES)
            ksw_ref[:, c * 256 + half * LANES:c * 256 + (half + 1) * LANES] = (
                _rope128(kv[:, sl], cos, sin).astype(BF16))
    vsw_ref[:, 0:256] = kv[:, 768:1024].astype(BF16)
    vsw_ref[:, 256:512] = kv[:, 1280:1536].astype(BF16)
    gate_ref[...] = _sigmoid(_dot(hb, wg_ref[...]))
    z = _dot(hb, wz_ref[...])
    sz_ref[...] = (z * _sigmoid(z)).astype(BF16)


def _nsa_proj(x2, g, wq, wkv, wg, wz, cos, sin, S, tm):
    n, d = x2.shape
    nt_s = S // tm
    row = lambda i: (i, 0)
    fix = lambda i: (0, 0)
    return pl.pallas_call(
        _nsa_proj_kernel,
        grid=(n // tm,),
        in_specs=[
            pl.BlockSpec((tm, d), row),
            pl.BlockSpec((1, d), fix),
            pl.BlockSpec(wq.shape, fix),
            pl.BlockSpec(wkv.shape, fix),
            pl.BlockSpec(wg.shape, fix),
            pl.BlockSpec(wz.shape, fix),
            pl.BlockSpec((tm, LANES), lambda i: (i % nt_s, 0)),
            pl.BlockSpec((tm, LANES), lambda i: (i % nt_s, 0)),
        ],
        out_specs=[
            pl.BlockSpec((tm, 1024), row),
            pl.BlockSpec((tm, 512), row),
            pl.BlockSpec((tm, 512), row),
            pl.BlockSpec((tm, 512), row),
            pl.BlockSpec((tm, LANES), row),
            pl.BlockSpec((tm, 1024), row),
        ],
        out_shape=[
            jax.ShapeDtypeStruct((n, 1024), BF16),
            jax.ShapeDtypeStruct((n, 512), F32),
            jax.ShapeDtypeStruct((n, 512), BF16),
            jax.ShapeDtypeStruct((n, 512), BF16),
            jax.ShapeDtypeStruct((n, LANES), F32),
            jax.ShapeDtypeStruct((n, 1024), BF16),
        ],
        compiler_params=_cparams("parallel"),
        name="nsa_proj",
    )(x2, g, wq, wkv, wg, wz, cos, sin)


def _compress_kernel(a_ref, pe_ref, w1_ref, w2_ref, w2p_ref, cos_ref, sin_ref, o_ref):
    a = a_ref[0, 0]
    nc = a.shape[0]
    half = a.shape[1]
    lo = _dot((a + pe_ref[0:1, :]).astype(BF16), w1_ref[0:half, :])
    hi = _dot((a + pe_ref[1:2, :]).astype(BF16), w1_ref[half:2 * half, :])
    hid = lo + pltpu.roll(hi, nc - 1, 0)
    act = (hid * _sigmoid(hid)).astype(BF16)
    raw = _dot(act, w2_ref[...])
    partner = _dot(act, w2p_ref[...])
    o_ref[0, 0] = (raw * cos_ref[...] + partner * sin_ref[...]).astype(BF16)


def _compress(a, pe2, w1, w2, w2p, cos, sin):
    B, G, nc, f = a.shape
    fix = lambda b, g: (0, 0)
    return pl.pallas_call(
        _compress_kernel,
        grid=(B, G),
        in_specs=[
            pl.BlockSpec((1, 1, nc, f), lambda b, g: (b, g, 0, 0)),
            pl.BlockSpec(pe2.shape, fix),
            pl.BlockSpec(w1.shape, fix),
            pl.BlockSpec(w2.shape, fix),
            pl.BlockSpec(w2p.shape, fix),
            pl.BlockSpec(cos.shape, fix),
            pl.BlockSpec(sin.shape, fix),
        ],
        out_specs=pl.BlockSpec((1, 1, nc, HEAD_DIM), lambda b, g: (b, g, 0, 0)),
        out_shape=jax.ShapeDtypeStruct((B, G, nc, HEAD_DIM), BF16),
        compiler_params=_cparams("parallel", "parallel"),
        name="compress",
    )(a, pe2, w1, w2, w2p, cos, sin)


def _cmp_select_kernel(q_ref, k_ref, v_ref, ovt_ref, o_ref, bias_ref, *, tq, n_top):
    i = pl.program_id(2)
    q = q_ref[0, 0, 0]
    rows = q.shape[0]
    nc = k_ref.shape[2]
    t0 = i * tq
    s = _dot_nt(q, k_ref[0, 0])
    t_row = t0 + lax.broadcasted_iota(jnp.int32, (rows, nc), 0) % tq
    n_col = lax.broadcasted_iota(jnp.int32, (rows, nc), 1)
    mask = n_col * CMP_STRIDE + (CMP_BLOCK - 1) <= t_row
    s = jnp.where(mask, s, NEG)
    m = jnp.max(s, axis=-1, keepdims=True)
    p = jnp.where(mask, jnp.exp(s - m), 0.0)
    l = jnp.sum(p, axis=-1, keepdims=True)
    p = p * jnp.where(l > 0.0, 1.0 / l, 0.0)
    o_ref[0, 0, 0] = _dot(p.astype(BF16), v_ref[0, 0]).astype(o_ref.dtype)

    ps = p[0:tq]
    for r in range(1, rows // tq):
        ps = ps + p[r * tq:(r + 1) * tq]
    ps_hi, ps_lo = _split_bf16(ps)
    ovt = ovt_ref[...]
    imp = _dot_nt(ovt, ps_hi) + _dot_nt(ovt, ps_lo)
    n_sel = imp.shape[0]
    j = lax.broadcasted_iota(jnp.int32, (n_sel, tq), 0)
    cur = (t0 + lax.broadcasted_iota(jnp.int32, (n_sel, tq), 1)) // SEL_BLOCK
    valid = j <= cur
    forced = (j == 0) | (j == cur) | (j == cur - 1)
    score = jnp.where(forced, 3e38, jnp.where(valid, imp, -1.0))
    jf = j.astype(F32)
    for _ in range(n_top):
        mx = jnp.max(score, axis=0, keepdims=True)
        pick = jnp.min(jnp.where(score == mx, jf, float(n_sel)), axis=0, keepdims=True)
        score = jnp.where(jf == pick, -2.0, score)
    keep = (score == -2.0) & valid
    bias_ref[0, 0, 0] = jnp.where(keep, 0.0, NEG).astype(BF16)


def _cmp_select(q5, kc, vc, ovt, tq, n_top):
    B, G, nt, rows, hd = q5.shape
    nc = kc.shape[2]
    n_sel = ovt.shape[0]
    return pl.pallas_call(
        functools.partial(_cmp_select_kernel, tq=tq, n_top=n_top),
        grid=(B, G, nt),
        in_specs=[
            pl.BlockSpec((1, 1, 1, rows, hd), lambda b, g, i: (b, g, i, 0, 0)),
            pl.BlockSpec((1, 1, nc, hd), lambda b, g, i: (b, g, 0, 0)),
            pl.BlockSpec((1, 1, nc, hd), lambda b, g, i: (b, g, 0, 0)),
            pl.BlockSpec(ovt.shape, lambda b, g, i: (0, 0)),
        ],
        out_specs=[
            pl.BlockSpec((1, 1, 1, rows, hd), lambda b, g, i: (b, g, i, 0, 0)),
            pl.BlockSpec((1, 1, 1, n_sel, tq), lambda b, g, i: (b, g, i, 0, 0)),
        ],
        out_shape=[
            jax.ShapeDtypeStruct((B, G, nt, rows, hd), BF16),
            jax.ShapeDtypeStruct((B, G, nt, n_sel, tq), BF16),
        ],
        compiler_params=_cparams("parallel", "parallel", "parallel"),
        name="cmp_select",
    )(q5, kc, vc, ovt)


def _sel_attn_kernel(q_ref, bias_ref, k_ref, v_ref, o_ref, *, tq, tk):
    i = pl.program_id(2)
    q = q_ref[0, 0, 0]
    rows = q.shape[0]
    reps = rows // tq
    t0 = i * tq
    n_kv = (t0 + tq + tk - 1) // tk
    tiles_per_group = SEL_GROUP * SEL_BLOCK // tk

    def scores(kt):
        b = bias_ref[0, 0, 0, kt // tiles_per_group]
        qa = q + jnp.concatenate([b] * reps, axis=0)
        k = k_ref[0, 0, pl.ds(pl.multiple_of(kt * tk, tk), tk), :]
        return _dot_nt(qa, k)

    def update(kt, s, m, acc):
        m_new = jnp.maximum(m, jnp.max(s, axis=-1, keepdims=True))
        alpha = jnp.exp(m - m_new)
        p = jnp.exp(s - m_new).astype(BF16)
        v = v_ref[0, 0, pl.ds(pl.multiple_of(kt * tk, tk), tk), :]
        return m_new, alpha * acc + _dot(p, v)

    def body(kt, carry):
        m, acc = carry
        return update(kt, scores(kt), m, acc)

    m0 = jnp.full((rows, 1), NEG, F32)
    acc0 = jnp.zeros((rows, LANES), F32)
    m, acc = lax.fori_loop(0, n_kv - 1, body, (m0, acc0))
    kt = n_kv - 1
    s = scores(kt)
    kpos = kt * tk + lax.broadcasted_iota(jnp.int32, (rows, tk), 1)
    t_row = t0 + lax.broadcasted_iota(jnp.int32, (rows, tk), 0) % tq
    s = jnp.where(kpos <= t_row, s, NEG)
    m, acc = update(kt, s, m, acc)
    o_ref[0, 0, 0] = (acc[:, 0:HEAD_DIM] / acc[:, HEAD_DIM:HEAD_DIM + 1]).astype(o_ref.dtype)


def _sel_attn(qp, biasp, ka, va, tq, tk):
    B, G, nt, rows, _ = qp.shape
    S = ka.shape[2]
    nsg = biasp.shape[3]
    return pl.pallas_call(
        functools.partial(_sel_attn_kernel, tq=tq, tk=tk),
        grid=(B, G, nt),
        in_specs=[
            pl.BlockSpec((1, 1, 1, rows, LANES), lambda b, g, i: (b, g, i, 0, 0)),
            pl.BlockSpec((1, 1, 1, nsg, tq, LANES), lambda b, g, i: (b, g, i, 0, 0, 0)),
            pl.BlockSpec((1, 1, S, LANES), lambda b, g, i: (b, g, 0, 0)),
            pl.BlockSpec((1, 1, S, LANES), lambda b, g, i: (b, g, 0, 0)),
        ],
        out_specs=pl.BlockSpec((1, 1, 1, rows, HEAD_DIM), lambda b, g, i: (b, g, i, 0, 0)),
        out_shape=jax.ShapeDtypeStruct((B, G, nt, rows, HEAD_DIM), BF16),
        compiler_params=_cparams("parallel", "parallel", "arbitrary"),
        name="sel_attn",
    )(qp, biasp, ka, va)


def _win_attn_kernel(q_ref, k_ref, v_ref, o_ref, *, tq):
    i = pl.program_id(2)
    q = q_ref[0, 0, 0]
    rows = q.shape[0]
    span = WINDOW + tq
    t0 = i * tq
    start = pl.multiple_of(jnp.maximum(t0 - WINDOW, 0), tq)
    k = k_ref[0, 0, pl.ds(start, span), :]
    v = v_ref[0, 0, pl.ds(start, span), :]
    s = _dot_nt(q, k)
    kpos = start + lax.broadcasted_iota(jnp.int32, (rows, span), 1)
    t_row = t0 + lax.broadcasted_iota(jnp.int32, (rows, span), 0) % tq
    dlt = t_row - kpos
    mask = (dlt >= 0) & (dlt < WINDOW)
    s = jnp.where(mask, s, NEG)
    m = jnp.max(s, axis=-1, keepdims=True)
    p = jnp.where(mask, jnp.exp(s - m), 0.0)
    l = jnp.sum(p, axis=-1, keepdims=True)
    o_ref[0, 0, 0] = (_dot(p.astype(BF16), v) / l).astype(o_ref.dtype)


def _win_attn(q5, kw, vw, tq):
    B, G, nt, rows, hd = q5.shape
    S = kw.shape[2]
    return pl.pallas_call(
        functools.partial(_win_attn_kernel, tq=tq),
        grid=(B, G, nt),
        in_specs=[
            pl.BlockSpec((1, 1, 1, rows, hd), lambda b, g, i: (b, g, i, 0, 0)),
            pl.BlockSpec((1, 1, S, hd), lambda b, g, i: (b, g, 0, 0)),
            pl.BlockSpec((1, 1, S, hd), lambda b, g, i: (b, g, 0, 0)),
        ],
        out_specs=pl.BlockSpec((1, 1, 1, rows, hd), lambda b, g, i: (b, g, i, 0, 0)),
        out_shape=jax.ShapeDtypeStruct((B, G, nt, rows, hd), BF16),
        compiler_params=_cparams("parallel", "parallel", "arbitrary"),
        name="win_attn",
    )(q5, kw, vw)


def _nsa_out_kernel(x_ref, oc_ref, os_ref, ow_ref, gate_ref, ex_ref, sz_ref, w_ref, y_ref):
    g_hi, g_lo = _split_bf16(gate_ref[...])
    o = None
    for c, o_ref in enumerate((oc_ref, os_ref, ow_ref)):
        e = ex_ref[c]
        term = (_dot(g_hi, e) + _dot(g_lo, e)) * o_ref[...].astype(F32)
        o = term if o is None else o + term
    u = (o * sz_ref[...].astype(F32)).astype(BF16)
    y_ref[...] = x_ref[...] + _dot(u, w_ref[...])


def _nsa_out(x2, oc, os_, ow, gates, expand, sz, w_out, tm):
    n, d = x2.shape
    row = lambda i: (i, 0)
    return pl.pallas_call(
        _nsa_out_kernel,
        grid=(n // tm,),
        in_specs=[
            pl.BlockSpec((tm, d), row),
            pl.BlockSpec((tm, d), row),
            pl.BlockSpec((tm, d), row),
            pl.BlockSpec((tm, d), row),
            pl.BlockSpec((tm, LANES), row),
            pl.BlockSpec(expand.shape, lambda i: (0, 0, 0)),
            pl.BlockSpec((tm, d), row),
            pl.BlockSpec(w_out.shape, lambda i: (0, 0)),
        ],
        out_specs=pl.BlockSpec((tm, d), row),
        out_shape=jax.ShapeDtypeStruct((n, d), F32),
        compiler_params=_cparams("parallel"),
        name="nsa_out",
    )(x2, oc, os_, ow, gates, expand, sz, w_out)


def _hg_proj_kernel(x_ref, g_ref, lb_ref, wq_ref, wf_ref, wi_ref, wz_ref,
                    q_ref, k_ref, lf_ref, v_ref, sz_ref):
    x = x_ref[...]
    h = x * lax.rsqrt(jnp.mean(x * x, axis=-1, keepdims=True) + EPS) * g_ref[...]
    hb = h.astype(BF16)
    qr = _dot(hb, wq_ref[...])
    q_ref[...] = (qr * _sigmoid(qr)).astype(BF16)
    lb = lb_ref[...]
    f = lb + (1.0 - lb) * _sigmoid(_dot(hb, wf_ref[...]))
    k_ref[...] = (1.0 - f).astype(BF16)
    lf_ref[...] = jnp.log(f)
    v_ref[...] = _dot(hb, wi_ref[...]).astype(BF16)
    z = _dot(hb, wz_ref[...])
    sz_ref[...] = (z * _sigmoid(z)).astype(BF16)


def _hg_proj(x2, g, lb, wq, wf, wi, wz, tm):
    n, d = x2.shape
    row = lambda i: (i, 0)
    fix = lambda i: (0, 0)
    return pl.pallas_call(
        _hg_proj_kernel,
        grid=(n // tm,),
        in_specs=[pl.BlockSpec((tm, d), row), pl.BlockSpec((1, d), fix), pl.BlockSpec((1, d), fix)]
        + [pl.BlockSpec((d, d), fix)] * 4,
        out_specs=[pl.BlockSpec((tm, d), row)] * 5,
        out_shape=[
            jax.ShapeDtypeStruct((n, d), BF16),
            jax.ShapeDtypeStruct((n, d), BF16),
            jax.ShapeDtypeStruct((n, d), F32),
            jax.ShapeDtypeStruct((n, d), BF16),
            jax.ShapeDtypeStruct((n, d), BF16),
        ],
        compiler_params=_cparams("parallel"),
        name="hg_proj",
    )(x2, g, lb, wq, wf, wi, wz)


def _hg_levels(c):
    t = np.arange(c)
    tmat, masks = [], []
    h = c
    while h >= 1:
        rstart = (t // h) * h
        rend = rstart + h
        tq = ((t[None, :] >= rstart[:, None]) & (t[None, :] <= t[:, None])).astype(np.float32)
        tk = ((t[None, :] > t[:, None]) & (t[None, :] < rend[:, None])).astype(np.float32)
        tmat += [tq, tk]
        if h < c:
            same = (t[:, None] // (2 * h)) == (t[None, :] // (2 * h))
            upper = ((t[:, None] // h) % 2) == 1
            lower = ((t[None, :] // h) % 2) == 0
            masks.append((same & upper & lower).astype(np.float32))
        h //= 2
    masks.append(np.eye(c, dtype=np.float32))
    return np.concatenate(tmat, axis=0), np.stack(masks, axis=0)


def _hg_scan_kernel(q_ref, k_ref, lf_ref, v_ref, gn_ref, tm_ref, mk_ref, o_ref, st_ref, *, c):
    @pl.when(pl.program_id(2) == 0)
    def _():
        st_ref[...] = jnp.zeros_like(st_ref)

    n_lvl = mk_ref.shape[0] - 1
    tmat = tm_ref[...]
    for ci in range(q_ref.shape[1] // c):
        sl = slice(ci * c, (ci + 1) * c)
        q = q_ref[0, sl, :].astype(F32)
        k = k_ref[0, sl, :].astype(F32)
        v = v_ref[0, sl, :]
        cum = jnp.dot(tmat, lf_ref[0, sl, :], precision=lax.Precision.HIGHEST,
                      preferred_element_type=F32)
        dec = jnp.exp(cum)
        part = lambda n: dec[n * c:(n + 1) * c]
        st = st_ref[...]
        o = _dot_nt((q * part(0)).astype(BF16), st.astype(BF16))
        att = mk_ref[n_lvl] * _dot_nt(q.astype(BF16), k.astype(BF16))
        for lv in range(n_lvl):
            qd = (q * part(2 * lv + 2)).astype(BF16)
            kd = (k * part(2 * lv + 3)).astype(BF16)
            att = att + mk_ref[lv] * _dot_nt(qd, kd)
        o = o + _dot(att.astype(BF16), v)
        b_all = part(0)[c - 1:c, :]
        st_ref[...] = st * b_all + _dot_tn(v, (k * part(1)).astype(BF16))
        on = o * lax.rsqrt(jnp.mean(o * o, axis=-1, keepdims=True) + EPS)
        o_ref[0, sl, :] = (on * gn_ref[...]).astype(o_ref.dtype)


def _hg_scan(q, k, lf, v, gn, tmat, masks, ct):
    B, S, d = q.shape
    nh = d // HG_HEAD_DIM
    blk = pl.BlockSpec((1, ct, HG_HEAD_DIM), lambda b, h, i: (b, i, h))
    return pl.pallas_call(
        functools.partial(_hg_scan_kernel, c=HG_CHUNK),
        grid=(B, nh, S // ct),
        in_specs=[blk, blk, blk, blk,
                  pl.BlockSpec((1, HG_HEAD_DIM), lambda b, h, i: (0, h)),
                  pl.BlockSpec(tmat.shape, lambda b, h, i: (0, 0)),
                  pl.BlockSpec(masks.shape, lambda b, h, i: (0, 0, 0))],
        out_specs=blk,
        out_shape=jax.ShapeDtypeStruct((B, S, d), BF16),
        scratch_shapes=[pltpu.VMEM((HG_HEAD_DIM, HG_HEAD_DIM), F32)],
        compiler_params=_cparams("parallel", "parallel", "arbitrary"),
        name="hg_scan",
    )(q, k, lf, v, gn, tmat, masks)


def _hg_out_kernel(x_ref, o_ref, sz_ref, w_ref, g_ref, y_ref):
    u = (o_ref[...].astype(F32) * sz_ref[...].astype(F32)).astype(BF16)
    x = x_ref[...] + _dot(u, w_ref[...])
    y_ref[...] = x * lax.rsqrt(jnp.mean(x * x, axis=-1, keepdims=True) + EPS) * g_ref[...]


def _hg_out(x2, o, sz, w_out, g, tm):
    n, d = x2.shape
    row = lambda i: (i, 0)
    fix = lambda i: (0, 0)
    return pl.pallas_call(
        _hg_out_kernel,
        grid=(n // tm,),
        in_specs=[pl.BlockSpec((tm, d), row), pl.BlockSpec((tm, d), row), pl.BlockSpec((tm, d), row),
                  pl.BlockSpec((d, d), fix), pl.BlockSpec((1, d), fix)],
        out_specs=pl.BlockSpec((tm, d), row),
        out_shape=jax.ShapeDtypeStruct((n, d), F32),
        compiler_params=_cparams("parallel"),
        name="hg_out",
    )(x2, o, sz, w_out, g)


def _nsa_layer(x2, B, S, g, w_in, pe_k, pe_v, wk1, wk2, wv1, wv2, w_out):
    H, G, R, hd = NSA_HEADS, NSA_GROUPS, NSA_HPG, HEAD_DIM
    n = B * S
    tm = 512
    tq = 128
    qw, kvw = H * hd, G * hd
    off = np.cumsum([0, qw] + [kvw] * 6 + [3 * H, qw])
    wq = w_in[:, off[0]:off[1]].astype(BF16)
    wkv = w_in[:, off[1]:off[7]].astype(BF16)
    wg = jnp.pad(w_in[:, off[7]:off[8]], ((0, 0), (0, LANES - 3 * H))).astype(BF16)
    wz = w_in[:, off[8]:off[9]].astype(BF16)
    pos = jnp.arange(S)
    cos, sin = _rope_tables(pos, LANES)
    q, kvc, ksw, vsw, gates, sz = _nsa_proj(x2, g[None, :], wq, wkv, wg, wz, cos, sin, S, tm)

    nc = S // CMP_STRIDE
    per = CMP_BLOCK // CMP_STRIDE
    assert per == 2
    def chunks(a):
        a = a.reshape(B, nc, CMP_STRIDE, G, hd).transpose(0, 3, 1, 2, 4)
        return a.reshape(B, G, nc, CMP_STRIDE * hd)
    cmp_end = jnp.arange(nc) * CMP_STRIDE + CMP_BLOCK - 1
    ccos, csin = _rope_tables(cmp_end, hd)
    swap = lambda w: jnp.concatenate([w[:, hd // 2:], w[:, :hd // 2]], axis=1)
    k_cmp = _compress(chunks(kvc[:, 0:kvw]), pe_k.reshape(per, CMP_STRIDE * hd), wk1.astype(BF16),
                      wk2.astype(BF16), swap(wk2).astype(BF16), ccos, csin)
    v_cmp = _compress(chunks(kvc[:, kvw:2 * kvw]), pe_v.reshape(per, CMP_STRIDE * hd), wv1.astype(BF16),
                      wv2.astype(BF16), swap(wv2).astype(BF16),
                      jnp.ones_like(ccos), jnp.zeros_like(csin))

    nt = S // tq
    q5 = q.reshape(B, nt, tq, G, R, hd).transpose(0, 3, 1, 4, 2, 5).reshape(B, G, nt, R * tq, hd)
    group_major = lambda a: a.reshape(B, S, G, hd).transpose(0, 2, 1, 3)
    ks, kw = group_major(ksw[:, 0:kvw]), group_major(ksw[:, kvw:2 * kvw])
    vs, vw = group_major(vsw[:, 0:kvw]), group_major(vsw[:, kvw:2 * kvw])

    n_sel = S // SEL_BLOCK
    n_top = min(SEL_TOPK, n_sel)
    cs = np.arange(nc)[:, None] * CMP_STRIDE
    ss = np.arange(n_sel)[None, :] * SEL_BLOCK
    overlap = ((cs < ss + SEL_BLOCK) & (cs + CMP_BLOCK > ss)).astype(np.float32)
    ovt = jnp.asarray(overlap.T, dtype=BF16)
    o_c, bias_t = _cmp_select(q5, k_cmp, v_cmp, ovt, tq, n_top)

    nsg = -(-n_sel // SEL_GROUP)
    bias = jnp.swapaxes(bias_t, 3, 4)
    bias = jnp.pad(bias, ((0, 0),) * 4 + ((0, nsg * SEL_GROUP - n_sel),), constant_values=NEG)
    bias = bias.reshape(B, G, nt, tq, nsg, SEL_GROUP).transpose(0, 1, 2, 4, 3, 5)
    biasp = jnp.concatenate([jnp.zeros_like(bias), bias], axis=-1)
    qp = jnp.concatenate([q5, jnp.zeros_like(q5)], axis=-1)
    onehot = jax.nn.one_hot((jnp.arange(S) // SEL_BLOCK) % SEL_GROUP, SEL_GROUP, dtype=BF16)
    ka = jnp.concatenate([ks, jnp.broadcast_to(onehot, (B, G, S, SEL_GROUP))], axis=-1)
    ones_col = jnp.zeros((S, LANES - hd), BF16).at[:, 0].set(1.0)
    va = jnp.concatenate([vs, jnp.broadcast_to(ones_col, (B, G, S, LANES - hd))], axis=-1)
    tk = min(512, S)
    o_s = _sel_attn(qp, biasp, ka, va, tq, tk)
    o_w = _win_attn(q5, kw, vw, tq)

    token_major = lambda o: o.reshape(B, G, nt, R, tq, hd).transpose(0, 2, 4, 1, 3, 5).reshape(n, qw)
    col = np.arange(3 * H)
    expand = np.zeros((3, LANES, qw), np.float32)
    for c in range(3):
        head = col[col % 3 == c] // 3
        for hh, cc in zip(head, col[col % 3 == c]):
            expand[c, cc, hh * hd:(hh + 1) * hd] = 1.0
    return _nsa_out(x2, token_major(o_c), token_major(o_s), token_major(o_w), gates,
                    jnp.asarray(expand, dtype=BF16), sz, w_out.astype(BF16), tm)


def _hgrn_layer(x2, B, S, g, w_in, lb, g_norm, w_out, final_g):
    n, d = x2.shape
    tm = 512
    wq, wf, wi, wz = (w_in[:, i * d:(i + 1) * d].astype(BF16) for i in range(4))
    q, k, lf, v, sz = _hg_proj(x2, g[None, :], lb[None, :], wq, wf, wi, wz, tm)
    tmat, masks = _hg_levels(HG_CHUNK)
    r3 = lambda a: a.reshape(B, S, d)
    o = _hg_scan(r3(q), r3(k), r3(lf), r3(v), g_norm[None, :], jnp.asarray(tmat), jnp.asarray(masks),
                 min(512, S))
    return _hg_out(x2, o.reshape(n, d), sz, w_out.astype(BF16), final_g[None, :], tm)


def kernel(x, norm_w, nsa_w_in, nsa_pe_k, nsa_pe_v, nsa_wk1, nsa_wk2, nsa_wv1, nsa_wv2,
           nsa_w_out, hg_w_in, hg_lb_logits, hg_norm, hg_w_out, final_norm):
    B, S, d = x.shape
    depth = norm_w.shape[0]
    assert depth == 2 and nsa_w_in.shape[0] == 1 and hg_w_in.shape[0] == 1
    p = jax.nn.softmax(hg_lb_logits.astype(F32), axis=0)
    lower_bounds = jnp.cumsum(p, axis=0) - p[0]
    x2 = x.reshape(B * S, d)
    x2 = _nsa_layer(x2, B, S, norm_w[0], nsa_w_in[0], nsa_pe_k[0], nsa_pe_v[0], nsa_wk1[0],
                    nsa_wk2[0], nsa_wv1[0], nsa_wv2[0], nsa_w_out[0])
    out = _hgrn_layer(x2, B, S, norm_w[1], hg_w_in[0], lower_bounds[1], hg_norm[0], hg_w_out[0],
                      final_norm)
    return out.reshape(B, S, d)
```

```python
import functools

import jax
import jax.numpy as jnp
import numpy as np
from jax import lax
from jax.experimental import pallas as pl
from jax.experimental.pallas import tpu as pltpu

EPS = 1e-6
ROPE_THETA = 10000.0

NSA_HEADS = 16
NSA_GROUPS = 4
NSA_HPG = NSA_HEADS // NSA_GROUPS
HEAD_DIM = 64
CMP_BLOCK = 32
CMP_STRIDE = 16
SEL_BLOCK = 64
SEL_TOPK = 16
WINDOW = 512

HG_HEAD_DIM = 128
HG_CHUNK = 64

NEG = -1e30
LOG2E = 1.4426950408889634
LANES = 128
SEL_GROUP = 64
VMEM_LIMIT = 52 * 1024 * 1024

F32 = jnp.float32
BF16 = jnp.bfloat16


def _cparams(*sem):
    return pltpu.CompilerParams(dimension_semantics=sem, vmem_limit_bytes=VMEM_LIMIT)


def _sigmoid(x):
    return 1.0 / (1.0 + jnp.exp(-x))


def _dot(a, b):
    return jnp.dot(a, b, preferred_element_type=F32)


def _dot_nt(a, b):
    return lax.dot_general(a, b, (((1,), (1,)), ((), ())), preferred_element_type=F32)


def _dot_tn(a, b):
    return lax.dot_general(a, b, (((0,), (0,)), ((), ())), preferred_element_type=F32)


def _split_bf16(x):
    hi = x.astype(BF16)
    lo = (x - hi.astype(F32)).astype(BF16)
    return hi, lo


def _rope_tables(pos, width):
    half = HEAD_DIM // 2
    inv = ROPE_THETA ** (-jnp.arange(half, dtype=F32) / half)
    ang = pos.astype(F32)[:, None] * inv[None, :]
    cos = jnp.concatenate([jnp.cos(ang), jnp.cos(ang)], axis=-1)
    sin = jnp.concatenate([-jnp.sin(ang), jnp.sin(ang)], axis=-1)
    reps = width // HEAD_DIM
    return jnp.tile(cos, (1, reps)), jnp.tile(sin, (1, reps))


def _rope128(x, cos, sin):
    lane = lax.broadcasted_iota(jnp.int32, x.shape, 1)
    first_half = (lane % HEAD_DIM) < (HEAD_DIM // 2)
    partner = jnp.where(first_half, pltpu.roll(x, LANES - HEAD_DIM // 2, 1),
                        pltpu.roll(x, HEAD_DIM // 2, 1))
    return x * cos + partner * sin


def _nsa_proj_kernel(x_ref, g_ref, wq_ref, wkv_ref, wg_ref, wz_ref, cos_ref, sin_ref,
                     q_ref, kvc_ref, ksw_ref, vsw_ref, gate_ref, sz_ref):
    x = x_ref[...]
    h = x * lax.rsqrt(jnp.mean(x * x, axis=-1, keepdims=True) + EPS) * g_ref[...]
    hb = h.astype(BF16)
    cos = cos_ref[...]
    sin = sin_ref[...]
    scale = HEAD_DIM ** -0.5 * LOG2E
    q = _dot(hb, wq_ref[...])
    for c in range(q.shape[1] // LANES):
        sl = slice(c * LANES, (c + 1) * LANES)
        q_ref[:, sl] = (_rope128(q[:, sl], cos, sin) * scale).astype(BF16)
    kv = _dot(hb, wkv_ref[...])
    kvc_ref[...] = kv[:, 0:512]
    for c, src in enumerate((2, 4)):
        for half in range(2):
            sl = slice(src * 256 + half * LANES, src * 256 + (half + 1) * LANES)
            ksw_ref[:, c * 256 + half * LANES:c * 256 + (half + 1) * LANES] = (
                _rope128(kv[:, sl], cos, sin).astype(BF16))
    vsw_ref[:, 0:256] = kv[:, 768:1024].astype(BF16)
    vsw_ref[:, 256:512] = kv[:, 1280:1536].astype(BF16)
    gate_ref[...] = _sigmoid(_dot(hb, wg_ref[...]))
    z = _dot(hb, wz_ref[...])
    sz_ref[...] = (z * _sigmoid(z)).astype(BF16)


def _nsa_proj(x2, g, wq, wkv, wg, wz, cos, sin, S, tm):
    n, d = x2.shape
    nt_s = S // tm
    row = lambda i: (i, 0)
    fix = lambda i: (0, 0)
    return pl.pallas_call(
        _nsa_proj_kernel,
        grid=(n // tm,),
        in_specs=[
            pl.BlockSpec((tm, d), row),
            pl.BlockSpec((1, d), fix),
            pl.BlockSpec(wq.shape, fix),
            pl.BlockSpec(wkv.shape, fix),
            pl.BlockSpec(wg.shape, fix),
            pl.BlockSpec(wz.shape, fix),
            pl.BlockSpec((tm, LANES), lambda i: (i % nt_s, 0)),
            pl.BlockSpec((tm, LANES), lambda i: (i % nt_s, 0)),
        ],
        out_specs=[
            pl.BlockSpec((tm, 1024), row),
            pl.BlockSpec((tm, 512), row),
            pl.BlockSpec((tm, 512), row),
            pl.BlockSpec((tm, 512), row),
            pl.BlockSpec((tm, LANES), row),
            pl.BlockSpec((tm, 1024), row),
        ],
        out_shape=[
            jax.ShapeDtypeStruct((n, 1024), BF16),
            jax.ShapeDtypeStruct((n, 512), F32),
            jax.ShapeDtypeStruct((n, 512), BF16),
            jax.ShapeDtypeStruct((n, 512), BF16),
            jax.ShapeDtypeStruct((n, LANES), F32),
            jax.ShapeDtypeStruct((n, 1024), BF16),
        ],
        compiler_params=_cparams("parallel"),
        name="nsa_proj",
    )(x2, g, wq, wkv, wg, wz, cos, sin)


def _compress_kernel(a_ref, pe_ref, w1_ref, w2_ref, w2p_ref, cos_ref, sin_ref, o_ref):
    a = a_ref[0, 0]
    nc = a.shape[0]
    half = a.shape[1]
    lo = _dot((a + pe_ref[0:1, :]).astype(BF16), w1_ref[0:half, :])
    hi = _dot((a + pe_ref[1:2, :]).astype(BF16), w1_ref[half:2 * half, :])
    hid = lo + pltpu.roll(hi, nc - 1, 0)
    act = (hid * _sigmoid(hid)).astype(BF16)
    raw = _dot(act, w2_ref[...])
    partner = _dot(act, w2p_ref[...])
    o_ref[0, 0] = (raw * cos_ref[...] + partner * sin_ref[...]).astype(BF16)


def _compress(a, pe2, w1, w2, w2p, cos, sin):
    B, G, nc, f = a.shape
    fix = lambda b, g: (0, 0)
    return pl.pallas_call(
        _compress_kernel,
        grid=(B, G),
        in_specs=[
            pl.BlockSpec((1, 1, nc, f), lambda b, g: (b, g, 0, 0)),
            pl.BlockSpec(pe2.shape, fix),
            pl.BlockSpec(w1.shape, fix),
            pl.BlockSpec(w2.shape, fix),
            pl.BlockSpec(w2p.shape, fix),
            pl.BlockSpec(cos.shape, fix),
            pl.BlockSpec(sin.shape, fix),
        ],
        out_specs=pl.BlockSpec((1, 1, nc, HEAD_DIM), lambda b, g: (b, g, 0, 0)),
        out_shape=jax.ShapeDtypeStruct((B, G, nc, HEAD_DIM), BF16),
        compiler_params=_cparams("parallel", "parallel"),
        name="compress",
    )(a, pe2, w1, w2, w2p, cos, sin)


def _cmp_select_kernel(q_ref, k_ref, v_ref, ovt_ref, o_ref, bias_ref, *, tq, n_top):
    i = pl.program_id(2)
    q = q_ref[0, 0, 0]
    rows = q.shape[0]
    nc = k_ref.shape[2]
    t0 = i * tq
    s = _dot_nt(q, k_ref[0, 0])
    t_row = t0 + lax.broadcasted_iota(jnp.int32, (rows, nc), 0) % tq
    n_col = lax.broadcasted_iota(jnp.int32, (rows, nc), 1)
    mask = n_col * CMP_STRIDE + (CMP_BLOCK - 1) <= t_row
    s = jnp.where(mask, s, NEG)
    m = jnp.max(s, axis=-1, keepdims=True)
    p = jnp.where(mask, jnp.exp2(s - m), 0.0)
    l = jnp.sum(p, axis=-1, keepdims=True)
    p = p * jnp.where(l > 0.0, 1.0 / l, 0.0)
    o_ref[0, 0, 0] = _dot(p.astype(BF16), v_ref[0, 0]).astype(o_ref.dtype)

    ps = p[0:tq]
    for r in range(1, rows // tq):
        ps = ps + p[r * tq:(r + 1) * tq]
    ps_hi, ps_lo = _split_bf16(ps)
    ovt = ovt_ref[...]
    imp = _dot_nt(ovt, ps_hi) + _dot_nt(ovt, ps_lo)
    n_sel = imp.shape[0]
    j = lax.broadcasted_iota(jnp.int32, (n_sel, tq), 0)
    cur = (t0 + lax.broadcasted_iota(jnp.int32, (n_sel, tq), 1)) // SEL_BLOCK
    valid = j <= cur
    forced = (j == 0) | (j == cur) | (j == cur - 1)
    score = jnp.where(forced, 3e38, jnp.where(valid, imp, -1.0))
    jf = j.astype(F32)
    for _ in range(n_top):
        mx = jnp.max(score, axis=0, keepdims=True)
        pick = jnp.min(jnp.where(score == mx, jf, float(n_sel)), axis=0, keepdims=True)
        score = jnp.where(jf == pick, -2.0, score)
    keep = (score == -2.0) & (j < t0 // SEL_BLOCK)
    bias_ref[0, 0, 0] = jnp.where(keep, 0.0, NEG).astype(BF16)


def _cmp_select(q5, kc, vc, ovt, tq, n_top):
    B, G, nt, rows, hd = q5.shape
    nc = kc.shape[2]
    n_sel = ovt.shape[0]
    return pl.pallas_call(
        functools.partial(_cmp_select_kernel, tq=tq, n_top=n_top),
        grid=(B, G, nt),
        in_specs=[
            pl.BlockSpec((1, 1, 1, rows, hd), lambda b, g, i: (b, g, i, 0, 0)),
            pl.BlockSpec((1, 1, nc, hd), lambda b, g, i: (b, g, 0, 0)),
            pl.BlockSpec((1, 1, nc, hd), lambda b, g, i: (b, g, 0, 0)),
            pl.BlockSpec(ovt.shape, lambda b, g, i: (0, 0)),
        ],
        out_specs=[
            pl.BlockSpec((1, 1, 1, rows, hd), lambda b, g, i: (b, g, i, 0, 0)),
            pl.BlockSpec((1, 1, 1, n_sel, tq), lambda b, g, i: (b, g, i, 0, 0)),
        ],
        out_shape=[
            jax.ShapeDtypeStruct((B, G, nt, rows, hd), BF16),
            jax.ShapeDtypeStruct((B, G, nt, n_sel, tq), BF16),
        ],
        compiler_params=_cparams("parallel", "parallel", "parallel"),
        name="cmp_select",
    )(q5, kc, vc, ovt)


def _sel_attn_kernel(q_ref, bias_ref, k_ref, v_ref, o_ref, s_ref, acc_ref, m_ref, *, tq, tk):
    i = pl.program_id(2)
    qt = q_ref[0, 0, 0]
    rows = qt.shape[1]
    reps = rows // tq
    t0 = pl.multiple_of(i * tq, tq)
    n_main = (t0 + tk - 1) // tk
    tiles_per_group = SEL_GROUP * SEL_BLOCK // tk

    def scores(kt):
        g0 = pl.multiple_of((kt // tiles_per_group) * SEL_GROUP, SEL_GROUP)
        b = bias_ref[0, 0, 0, pl.ds(g0, SEL_GROUP), :]
        qa = jnp.concatenate([qt, jnp.concatenate([b] * reps, axis=1)], axis=0)
        k = k_ref[0, 0, pl.ds(pl.multiple_of(kt * tk, tk), tk), :]
        return _dot(k, qa)

    def update(slot, kt):
        s = s_ref[slot]
        m = m_ref[...]
        m_new = jnp.maximum(m, jnp.max(s, axis=0, keepdims=True))
        p = jnp.exp2(s - m_new).astype(BF16)
        v = v_ref[0, 0, :, pl.ds(pl.multiple_of(kt * tk, tk), tk)]
        acc_ref[...] = jnp.exp2(m - m_new) * acc_ref[...] + _dot(v, p)
        m_ref[...] = m_new

    s = _dot(k_ref[0, 0, pl.ds(t0, tq), 0:HEAD_DIM], qt)
    kpos = t0 + lax.broadcasted_iota(jnp.int32, (tq, rows), 0)
    t_col = t0 + lax.broadcasted_iota(jnp.int32, (tq, rows), 1) % tq
    s = jnp.where(kpos <= t_col, s, NEG)
    m = jnp.max(s, axis=0, keepdims=True)
    m_ref[...] = m
    acc_ref[...] = _dot(v_ref[0, 0, :, pl.ds(t0, tq)], jnp.exp2(s - m).astype(BF16))

    s_ref[0] = scores(0)

    def body(jj, carry):
        k0 = 2 * jj
        s_ref[1] = scores(k0 + 1)
        update(0, k0)
        s_ref[0] = scores(jnp.minimum(k0 + 2, n_main - 1))
        update(1, k0 + 1)
        return carry

    lax.fori_loop(0, n_main // 2, body, 0)

    @pl.when(n_main % 2 == 1)
    def _():
        update(0, n_main - 1)

    o_ref[0, 0, 0] = (acc_ref[0:HEAD_DIM, :] / acc_ref[HEAD_DIM:HEAD_DIM + 1, :]).astype(o_ref.dtype)


def _sel_attn(qt, bias_t, ka, vat, tq, tk):
    B, G, nt, hd, rows = qt.shape
    S = ka.shape[2]
    nb = bias_t.shape[3]
    return pl.pallas_call(
        functools.partial(_sel_attn_kernel, tq=tq, tk=tk),
        grid=(B, G, nt),
        in_specs=[
            pl.BlockSpec((1, 1, 1, hd, rows), lambda b, g, i: (b, g, i, 0, 0)),
            pl.BlockSpec((1, 1, 1, nb, tq), lambda b, g, i: (b, g, i, 0, 0)),
            pl.BlockSpec((1, 1, S, LANES), lambda b, g, i: (b, g, 0, 0)),
            pl.BlockSpec((1, 1, LANES, S), lambda b, g, i: (b, g, 0, 0)),
        ],
        out_specs=pl.BlockSpec((1, 1, 1, hd, rows), lambda b, g, i: (b, g, i, 0, 0)),
        out_shape=jax.ShapeDtypeStruct((B, G, nt, hd, rows), BF16),
        scratch_shapes=[pltpu.VMEM((2, tk, rows), F32),
                        pltpu.VMEM((LANES, rows), F32),
                        pltpu.VMEM((1, rows), F32)],
        compiler_params=_cparams("parallel", "parallel", "arbitrary"),
        name="sel_attn",
    )(qt, bias_t, ka, vat)


def _win_attn_kernel(q_ref, k_ref, v_ref, o_ref, *, tq):
    i = pl.program_id(2)
    q = q_ref[0, 0, 0]
    rows = q.shape[0]
    span = WINDOW + tq
    t0 = i * tq
    start = pl.multiple_of(jnp.maximum(t0 - WINDOW, 0), tq)
    k = k_ref[0, 0, pl.ds(start, span), :]
    v = v_ref[0, 0, pl.ds(start, span), :]
    s = _dot_nt(q, k)
    kpos = start + lax.broadcasted_iota(jnp.int32, (rows, span), 1)
    t_row = t0 + lax.broadcasted_iota(jnp.int32, (rows, span), 0) % tq
    dlt = t_row - kpos
    mask = (dlt >= 0) & (dlt < WINDOW)
    s = jnp.where(mask, s, NEG)
    m = jnp.max(s, axis=-1, keepdims=True)
    p = jnp.where(mask, jnp.exp2(s - m), 0.0)
    l = jnp.sum(p, axis=-1, keepdims=True)
    o_ref[0, 0, 0] = (_dot(p.astype(BF16), v) / l).astype(o_ref.dtype)


def _win_attn(q5, kw, vw, tq):
    B, G, nt, rows, hd = q5.shape
    S = kw.shape[2]
    return pl.pallas_call(
        functools.partial(_win_attn_kernel, tq=tq),
        grid=(B, G, nt),
        in_specs=[
            pl.BlockSpec((1, 1, 1, rows, hd), lambda b, g, i: (b, g, i, 0, 0)),
            pl.BlockSpec((1, 1, S, hd), lambda b, g, i: (b, g, 0, 0)),
            pl.BlockSpec((1, 1, S, hd), lambda b, g, i: (b, g, 0, 0)),
        ],
        out_specs=pl.BlockSpec((1, 1, 1, rows, hd), lambda b, g, i: (b, g, i, 0, 0)),
        out_shape=jax.ShapeDtypeStruct((B, G, nt, rows, hd), BF16),
        compiler_params=_cparams("parallel", "parallel", "arbitrary"),
        name="win_attn",
    )(q5, kw, vw)


def _nsa_out_kernel(x_ref, oc_ref, os_ref, ow_ref, gate_ref, ex_ref, sz_ref, w_ref, y_ref):
    g_hi, g_lo = _split_bf16(gate_ref[...])
    o = None
    for c, o_ref in enumerate((oc_ref, os_ref, ow_ref)):
        e = ex_ref[c]
        term = (_dot(g_hi, e) + _dot(g_lo, e)) * o_ref[...].astype(F32)
        o = term if o is None else o + term
    u = (o * sz_ref[...].astype(F32)).astype(BF16)
    y_ref[...] = x_ref[...] + _dot(u, w_ref[...])


def _nsa_out(x2, oc, os_, ow, gates, expand, sz, w_out, tm):
    n, d = x2.shape
    row = lambda i: (i, 0)
    return pl.pallas_call(
        _nsa_out_kernel,
        grid=(n // tm,),
        in_specs=[
            pl.BlockSpec((tm, d), row),
            pl.BlockSpec((tm, d), row),
            pl.BlockSpec((tm, d), row),
            pl.BlockSpec((tm, d), row),
            pl.BlockSpec((tm, LANES), row),
            pl.BlockSpec(expand.shape, lambda i: (0, 0, 0)),
            pl.BlockSpec((tm, d), row),
            pl.BlockSpec(w_out.shape, lambda i: (0, 0)),
        ],
        out_specs=pl.BlockSpec((tm, d), row),
        out_shape=jax.ShapeDtypeStruct((n, d), F32),
        compiler_params=_cparams("parallel"),
        name="nsa_out",
    )(x2, oc, os_, ow, gates, expand, sz, w_out)


def _hg_proj_kernel(x_ref, g_ref, lb_ref, wq_ref, wf_ref, wi_ref, wz_ref,
                    q_ref, k_ref, lf_ref, v_ref, sz_ref):
    x = x_ref[...]
    h = x * lax.rsqrt(jnp.mean(x * x, axis=-1, keepdims=True) + EPS) * g_ref[...]
    hb = h.astype(BF16)
    qr = _dot(hb, wq_ref[...])
    q_ref[...] = (qr * _sigmoid(qr)).astype(BF16)
    lb = lb_ref[...]
    f = lb + (1.0 - lb) * _sigmoid(_dot(hb, wf_ref[...]))
    k_ref[...] = (1.0 - f).astype(BF16)
    lf_ref[...] = jnp.log(f)
    v_ref[...] = _dot(hb, wi_ref[...]).astype(BF16)
    z = _dot(hb, wz_ref[...])
    sz_ref[...] = (z * _sigmoid(z)).astype(BF16)


def _hg_proj(x2, g, lb, wq, wf, wi, wz, tm):
    n, d = x2.shape
    row = lambda i: (i, 0)
    fix = lambda i: (0, 0)
    return pl.pallas_call(
        _hg_proj_kernel,
        grid=(n // tm,),
        in_specs=[pl.BlockSpec((tm, d), row), pl.BlockSpec((1, d), fix), pl.BlockSpec((1, d), fix)]
        + [pl.BlockSpec((d, d), fix)] * 4,
        out_specs=[pl.BlockSpec((tm, d), row)] * 5,
        out_shape=[
            jax.ShapeDtypeStruct((n, d), BF16),
            jax.ShapeDtypeStruct((n, d), BF16),
            jax.ShapeDtypeStruct((n, d), F32),
            jax.ShapeDtypeStruct((n, d), BF16),
            jax.ShapeDtypeStruct((n, d), BF16),
        ],
        compiler_params=_cparams("parallel"),
        name="hg_proj",
    )(x2, g, lb, wq, wf, wi, wz)


def _hg_masks(c):
    t = np.arange(c)
    masks = []
    h = 1
    while h < c:
        same = (t[:, None] // (2 * h)) == (t[None, :] // (2 * h))
        upper = ((t[:, None] // h) % 2) == 1
        lower = ((t[None, :] // h) % 2) == 0
        masks.append((same & upper & lower).astype(np.float32))
        h *= 2
    masks.append(np.eye(c, dtype=np.float32))
    return np.stack(masks, axis=0)


def _hg_scan_kernel(q_ref, k_ref, lf_ref, v_ref, gn_ref, mk_ref, o_ref, st_ref, qd_ref, kd_ref, *, c):
    @pl.when(pl.program_id(2) == 0)
    def _():
        st_ref[...] = jnp.zeros_like(st_ref)

    ct = q_ref.shape[1]
    n_lvl = mk_ref.shape[0] - 1
    lf = lf_ref[0]
    pos = lax.broadcasted_iota(jnp.int32, lf.shape, 0) % c
    b = lf
    d = 1
    while d < c:
        b = b + jnp.where(pos >= d, pltpu.roll(b, d, 0), 0.0)
        d *= 2
    q = q_ref[0].astype(F32)
    k = k_ref[0].astype(F32)
    start, end = b - lf, b
    h = 1
    for lv in range(n_lvl + 1):
        qd_ref[lv] = (q * jnp.exp(b - start)).astype(BF16)
        kd_ref[lv] = (k * jnp.exp(end - b)).astype(BF16)
        if h < c:
            odd = (pos // h) % 2 == 1
            start = jnp.where(odd, pltpu.roll(start, h, 0), start)
            end = jnp.where(odd, end, pltpu.roll(end, ct - h, 0))
            h *= 2
    chunk_decay = jnp.exp(end)

    st = st_ref[...]
    for ci in range(ct // c):
        sl = slice(ci * c, (ci + 1) * c)
        v = v_ref[0, sl, :]
        att = mk_ref[n_lvl] * _dot_nt(q_ref[0, sl, :], k_ref[0, sl, :])
        for lv in range(n_lvl):
            att = att + mk_ref[lv] * _dot_nt(qd_ref[lv, sl, :], kd_ref[lv, sl, :])
        o = _dot_nt(qd_ref[n_lvl, sl, :], st.astype(BF16)) + _dot(att.astype(BF16), v)
        st = st * chunk_decay[ci * c:ci * c + 1, :] + _dot_tn(v, kd_ref[n_lvl, sl, :])
        on = o * lax.rsqrt(jnp.mean(o * o, axis=-1, keepdims=True) + EPS)
        o_ref[0, sl, :] = (on * gn_ref[...]).astype(o_ref.dtype)
    st_ref[...] = st


def _hg_scan(q, k, lf, v, gn, masks, ct):
    B, S, d = q.shape
    nh = d // HG_HEAD_DIM
    n_parts = masks.shape[0]
    blk = pl.BlockSpec((1, ct, HG_HEAD_DIM), lambda b, h, i: (b, i, h))
    return pl.pallas_call(
        functools.partial(_hg_scan_kernel, c=HG_CHUNK),
        grid=(B, nh, S // ct),
        in_specs=[blk, blk, blk, blk,
                  pl.BlockSpec((1, HG_HEAD_DIM), lambda b, h, i: (0, h)),
                  pl.BlockSpec(masks.shape, lambda b, h, i: (0, 0, 0))],
        out_specs=blk,
        out_shape=jax.ShapeDtypeStruct((B, S, d), BF16),
        scratch_shapes=[pltpu.VMEM((HG_HEAD_DIM, HG_HEAD_DIM), F32),
                        pltpu.VMEM((n_parts, ct, HG_HEAD_DIM), BF16),
                        pltpu.VMEM((n_parts, ct, HG_HEAD_DIM), BF16)],
        compiler_params=_cparams("parallel", "parallel", "arbitrary"),
        name="hg_scan",
    )(q, k, lf, v, gn, masks)


def _hg_out_kernel(x_ref, o_ref, sz_ref, w_ref, g_ref, y_ref):
    u = (o_ref[...].astype(F32) * sz_ref[...].astype(F32)).astype(BF16)
    x = x_ref[...] + _dot(u, w_ref[...])
    y_ref[...] = x * lax.rsqrt(jnp.mean(x * x, axis=-1, keepdims=True) + EPS) * g_ref[...]


def _hg_out(x2, o, sz, w_out, g, tm):
    n, d = x2.shape
    row = lambda i: (i, 0)
    fix = lambda i: (0, 0)
    return pl.pallas_call(
        _hg_out_kernel,
        grid=(n // tm,),
        in_specs=[pl.BlockSpec((tm, d), row), pl.BlockSpec((tm, d), row), pl.BlockSpec((tm, d), row),
                  pl.BlockSpec((d, d), fix), pl.BlockSpec((1, d), fix)],
        out_specs=pl.BlockSpec((tm, d), row),
        out_shape=jax.ShapeDtypeStruct((n, d), F32),
        compiler_params=_cparams("parallel"),
        name="hg_out",
    )(x2, o, sz, w_out, g)


def _nsa_layer(x2, B, S, g, w_in, pe_k, pe_v, wk1, wk2, wv1, wv2, w_out):
    H, G, R, hd = NSA_HEADS, NSA_GROUPS, NSA_HPG, HEAD_DIM
    n = B * S
    tm = 512
    tq = 128
    qw, kvw = H * hd, G * hd
    off = np.cumsum([0, qw] + [kvw] * 6 + [3 * H, qw])
    wq = w_in[:, off[0]:off[1]].astype(BF16)
    wkv = w_in[:, off[1]:off[7]].astype(BF16)
    wg = jnp.pad(w_in[:, off[7]:off[8]], ((0, 0), (0, LANES - 3 * H))).astype(BF16)
    wz = w_in[:, off[8]:off[9]].astype(BF16)
    pos = jnp.arange(S)
    cos, sin = _rope_tables(pos, LANES)
    q, kvc, ksw, vsw, gates, sz = _nsa_proj(x2, g[None, :], wq, wkv, wg, wz, cos, sin, S, tm)

    nc = S // CMP_STRIDE
    per = CMP_BLOCK // CMP_STRIDE
    assert per == 2
    def chunks(a):
        a = a.reshape(B, nc, CMP_STRIDE, G, hd).transpose(0, 3, 1, 2, 4)
        return a.reshape(B, G, nc, CMP_STRIDE * hd)
    cmp_end = jnp.arange(nc) * CMP_STRIDE + CMP_BLOCK - 1
    ccos, csin = _rope_tables(cmp_end, hd)
    swap = lambda w: jnp.concatenate([w[:, hd // 2:], w[:, :hd // 2]], axis=1)
    k_cmp = _compress(chunks(kvc[:, 0:kvw]), pe_k.reshape(per, CMP_STRIDE * hd), wk1.astype(BF16),
                      wk2.astype(BF16), swap(wk2).astype(BF16), ccos, csin)
    v_cmp = _compress(chunks(kvc[:, kvw:2 * kvw]), pe_v.reshape(per, CMP_STRIDE * hd), wv1.astype(BF16),
                      wv2.astype(BF16), swap(wv2).astype(BF16),
                      jnp.ones_like(ccos), jnp.zeros_like(csin))

    nt = S // tq
    q5 = q.reshape(B, nt, tq, G, R, hd).transpose(0, 3, 1, 4, 2, 5).reshape(B, G, nt, R * tq, hd)
    group_major = lambda a: a.reshape(B, S, G, hd).transpose(0, 2, 1, 3)
    ks, kw = group_major(ksw[:, 0:kvw]), group_major(ksw[:, kvw:2 * kvw])
    vs, vw = group_major(vsw[:, 0:kvw]), group_major(vsw[:, kvw:2 * kvw])

    n_sel = S // SEL_BLOCK
    n_top = min(SEL_TOPK, n_sel)
    cs = np.arange(nc)[:, None] * CMP_STRIDE
    ss = np.arange(n_sel)[None, :] * SEL_BLOCK
    overlap = ((cs < ss + SEL_BLOCK) & (cs + CMP_BLOCK > ss)).astype(np.float32)
    ovt = jnp.asarray(overlap.T, dtype=BF16)
    o_c, bias_t = _cmp_select(q5, k_cmp, v_cmp, ovt, tq, n_top)

    nsg = -(-n_sel // SEL_GROUP)
    bias_t = jnp.pad(bias_t, ((0, 0),) * 3 + ((0, nsg * SEL_GROUP - n_sel), (0, 0)), constant_values=NEG)
    qt = q.reshape(B, nt, tq, G, R, hd).transpose(0, 3, 1, 5, 4, 2).reshape(B, G, nt, hd, R * tq)
    onehot = jax.nn.one_hot((jnp.arange(S) // SEL_BLOCK) % SEL_GROUP, SEL_GROUP, dtype=BF16)
    ka = jnp.concatenate([ks, jnp.broadcast_to(onehot, (B, G, S, SEL_GROUP))], axis=-1)
    ones_row = jnp.zeros((LANES - hd, S), BF16).at[0, :].set(1.0)
    vat = jnp.concatenate([jnp.swapaxes(vs, 2, 3), jnp.broadcast_to(ones_row, (B, G, LANES - hd, S))], axis=2)
    tk = min(512, S)
    o_s = _sel_attn(qt, bias_t, ka, vat, tq, tk)
    o_s = o_s.reshape(B, G, nt, hd, R, tq).transpose(0, 2, 5, 1, 4, 3).reshape(n, qw)
    o_w = _win_attn(q5, kw, vw, tq)

    token_major = lambda o: o.reshape(B, G, nt, R, tq, hd).transpose(0, 2, 4, 1, 3, 5).reshape(n, qw)
    col = np.arange(3 * H)
    expand = np.zeros((3, LANES, qw), np.float32)
    for c in range(3):
        head = col[col % 3 == c] // 3
        for hh, cc in zip(head, col[col % 3 == c]):
            expand[c, cc, hh * hd:(hh + 1) * hd] = 1.0
    return _nsa_out(x2, token_major(o_c), o_s, token_major(o_w), gates,
                    jnp.asarray(expand, dtype=BF16), sz, w_out.astype(BF16), tm)


def _hgrn_layer(x2, B, S, g, w_in, lb, g_norm, w_out, final_g):
    n, d = x2.shape
    tm = 512
    wq, wf, wi, wz = (w_in[:, i * d:(i + 1) * d].astype(BF16) for i in range(4))
    q, k, lf, v, sz = _hg_proj(x2, g[None, :], lb[None, :], wq, wf, wi, wz, tm)
    r3 = lambda a: a.reshape(B, S, d)
    o = _hg_scan(r3(q), r3(k), r3(lf), r3(v), g_norm[None, :], jnp.asarray(_hg_masks(HG_CHUNK)),
                 min(512, S))
    return _hg_out(x2, o.reshape(n, d), sz, w_out.astype(BF16), final_g[None, :], tm)


def kernel(x, norm_w, nsa_w_in, nsa_pe_k, nsa_pe_v, nsa_wk1, nsa_wk2, nsa_wv1, nsa_wv2,
           nsa_w_out, hg_w_in, hg_lb_logits, hg_norm, hg_w_out, final_norm):
    B, S, d = x.shape
    depth = norm_w.shape[0]
    assert depth == 2 and nsa_w_in.shape[0] == 1 and hg_w_in.shape[0] == 1
    p = jax.nn.softmax(hg_lb_logits.astype(F32), axis=0)
    lower_bounds = jnp.cumsum(p, axis=0) - p[0]
    x2 = x.reshape(B * S, d)
    x2 = _nsa_layer(x2, B, S, norm_w[0], nsa_w_in[0], nsa_pe_k[0], nsa_pe_v[0], nsa_wk1[0],
                    nsa_wk2[0], nsa_wv1[0], nsa_wv2[0], nsa_w_out[0])
    out = _hgrn_layer(x2, B, S, norm_w[1], hg_w_in[0], lower_bounds[1], hg_norm[0], hg_w_out[0],
                      final_norm)
    return out.reshape(B, S, d)
```

```python
import functools

import jax
import jax.numpy as jnp
import numpy as np
from jax import lax
from jax.experimental import pallas as pl
from jax.experimental.pallas import tpu as pltpu

EPS = 1e-6
ROPE_THETA = 10000.0

NSA_HEADS = 16
NSA_GROUPS = 4
NSA_HPG = NSA_HEADS // NSA_GROUPS
HEAD_DIM = 64
CMP_BLOCK = 32
CMP_STRIDE = 16
SEL_BLOCK = 64
SEL_TOPK = 16
WINDOW = 512

HG_HEAD_DIM = 128
HG_CHUNK = 64

NEG = -1e30
LOG2E = 1.4426950408889634
LANES = 128
SEL_GROUP = 64
GROUP_PAIR = 2
PV_ROWS = HEAD_DIM + 16
VMEM_LIMIT = 52 * 1024 * 1024

F32 = jnp.float32
BF16 = jnp.bfloat16


def _cparams(*sem):
    return pltpu.CompilerParams(dimension_semantics=sem, vmem_limit_bytes=VMEM_LIMIT)


def _sigmoid(x):
    return 1.0 / (1.0 + jnp.exp(-x))


def _dot(a, b):
    return jnp.dot(a, b, preferred_element_type=F32)


def _dot_nt(a, b):
    return lax.dot_general(a, b, (((1,), (1,)), ((), ())), preferred_element_type=F32)


def _dot_tn(a, b):
    return lax.dot_general(a, b, (((0,), (0,)), ((), ())), preferred_element_type=F32)


def _split_bf16(x):
    hi = x.astype(BF16)
    lo = (x - hi.astype(F32)).astype(BF16)
    return hi, lo


def _rope_tables(pos, width):
    half = HEAD_DIM // 2
    inv = ROPE_THETA ** (-jnp.arange(half, dtype=F32) / half)
    ang = pos.astype(F32)[:, None] * inv[None, :]
    cos = jnp.concatenate([jnp.cos(ang), jnp.cos(ang)], axis=-1)
    sin = jnp.concatenate([-jnp.sin(ang), jnp.sin(ang)], axis=-1)
    reps = width // HEAD_DIM
    return jnp.tile(cos, (1, reps)), jnp.tile(sin, (1, reps))


def _rope128(x, cos, sin):
    lane = lax.broadcasted_iota(jnp.int32, x.shape, 1)
    first_half = (lane % HEAD_DIM) < (HEAD_DIM // 2)
    partner = jnp.where(first_half, pltpu.roll(x, LANES - HEAD_DIM // 2, 1),
                        pltpu.roll(x, HEAD_DIM // 2, 1))
    return x * cos + partner * sin


def _nsa_proj_kernel(x_ref, g_ref, wq_ref, wkv_ref, wg_ref, wz_ref, cos_ref, sin_ref,
                     q_ref, kvc_ref, ksw_ref, vsw_ref, gate_ref, sz_ref):
    x = x_ref[...]
    h = x * lax.rsqrt(jnp.mean(x * x, axis=-1, keepdims=True) + EPS) * g_ref[...]
    hb = h.astype(BF16)
    cos = cos_ref[...]
    sin = sin_ref[...]
    scale = HEAD_DIM ** -0.5 * LOG2E
    q = _dot(hb, wq_ref[...])
    for c in range(q.shape[1] // LANES):
        sl = slice(c * LANES, (c + 1) * LANES)
        q_ref[:, sl] = (_rope128(q[:, sl], cos, sin) * scale).astype(BF16)
    kv = _dot(hb, wkv_ref[...])
    kvc_ref[...] = kv[:, 0:512]
    for c, src in enumerate((2, 4)):
        for half in range(2):
            sl = slice(src * 256 + half * LANES, src * 256 + (half + 1) * LANES)
            ksw_ref[:, c * 256 + half * LANES:c * 256 + (half + 1) * LANES] = (
                _rope128(kv[:, sl], cos, sin).astype(BF16))
    vsw_ref[:, 0:256] = kv[:, 768:1024].astype(BF16)
    vsw_ref[:, 256:512] = kv[:, 1280:1536].astype(BF16)
    gate_ref[...] = _sigmoid(_dot(hb, wg_ref[...]))
    z = _dot(hb, wz_ref[...])
    sz_ref[...] = (z * _sigmoid(z)).astype(BF16)


def _nsa_proj(x2, g, wq, wkv, wg, wz, cos, sin, S, tm):
    n, d = x2.shape
    nt_s = S // tm
    row = lambda i: (i, 0)
    fix = lambda i: (0, 0)
    return pl.pallas_call(
        _nsa_proj_kernel,
        grid=(n // tm,),
        in_specs=[
            pl.BlockSpec((tm, d), row),
            pl.BlockSpec((1, d), fix),
            pl.BlockSpec(wq.shape, fix),
            pl.BlockSpec(wkv.shape, fix),
            pl.BlockSpec(wg.shape, fix),
            pl.BlockSpec(wz.shape, fix),
            pl.BlockSpec((tm, LANES), lambda i: (i % nt_s, 0)),
            pl.BlockSpec((tm, LANES), lambda i: (i % nt_s, 0)),
        ],
        out_specs=[
            pl.BlockSpec((tm, 1024), row),
            pl.BlockSpec((tm, 512), row),
            pl.BlockSpec((tm, 512), row),
            pl.BlockSpec((tm, 512), row),
            pl.BlockSpec((tm, LANES), row),
            pl.BlockSpec((tm, 1024), row),
        ],
        out_shape=[
            jax.ShapeDtypeStruct((n, 1024), BF16),
            jax.ShapeDtypeStruct((n, 512), F32),
            jax.ShapeDtypeStruct((n, 512), BF16),
            jax.ShapeDtypeStruct((n, 512), BF16),
            jax.ShapeDtypeStruct((n, LANES), F32),
            jax.ShapeDtypeStruct((n, 1024), BF16),
        ],
        compiler_params=_cparams("parallel"),
        name="nsa_proj",
    )(x2, g, wq, wkv, wg, wz, cos, sin)


def _compress_kernel(a_ref, pe_ref, w1_ref, w2_ref, w2p_ref, cos_ref, sin_ref, o_ref):
    a = a_ref[0, 0]
    nc = a.shape[0]
    half = a.shape[1]
    lo = _dot((a + pe_ref[0:1, :]).astype(BF16), w1_ref[0:half, :])
    hi = _dot((a + pe_ref[1:2, :]).astype(BF16), w1_ref[half:2 * half, :])
    hid = lo + pltpu.roll(hi, nc - 1, 0)
    act = (hid * _sigmoid(hid)).astype(BF16)
    raw = _dot(act, w2_ref[...])
    partner = _dot(act, w2p_ref[...])
    o_ref[0, 0] = (raw * cos_ref[...] + partner * sin_ref[...]).astype(BF16)


def _compress(a, pe2, w1, w2, w2p, cos, sin):
    B, G, nc, f = a.shape
    fix = lambda b, g: (0, 0)
    return pl.pallas_call(
        _compress_kernel,
        grid=(B, G),
        in_specs=[
            pl.BlockSpec((1, 1, nc, f), lambda b, g: (b, g, 0, 0)),
            pl.BlockSpec(pe2.shape, fix),
            pl.BlockSpec(w1.shape, fix),
            pl.BlockSpec(w2.shape, fix),
            pl.BlockSpec(w2p.shape, fix),
            pl.BlockSpec(cos.shape, fix),
            pl.BlockSpec(sin.shape, fix),
        ],
        out_specs=pl.BlockSpec((1, 1, nc, HEAD_DIM), lambda b, g: (b, g, 0, 0)),
        out_shape=jax.ShapeDtypeStruct((B, G, nc, HEAD_DIM), BF16),
        compiler_params=_cparams("parallel", "parallel"),
        name="compress",
    )(a, pe2, w1, w2, w2p, cos, sin)


def _cmp_select_kernel(q_ref, k_ref, v_ref, ovt_ref, o_ref, bias_ref, *, tq, n_top):
    for gg in range(q_ref.shape[1]):
        _cmp_select_group(q_ref.at[0, gg, 0], k_ref.at[0, gg], v_ref.at[0, gg], ovt_ref,
                          o_ref.at[:, gg * NSA_HPG * HEAD_DIM:(gg + 1) * NSA_HPG * HEAD_DIM],
                          bias_ref.at[0, gg, 0], tq=tq, n_top=n_top)


def _cmp_select_group(q_ref, k_ref, v_ref, ovt_ref, o_ref, bias_ref, *, tq, n_top):
    i = pl.program_id(2)
    q = q_ref[...]
    rows = q.shape[0]
    nc = k_ref.shape[0]
    t0 = i * tq
    s = _dot_nt(q, k_ref[...])
    t_col = t0 + lax.broadcasted_iota(jnp.int32, (rows, 1), 0) % tq
    last_ok = (t_col - (CMP_BLOCK - 1)) // CMP_STRIDE
    n_row = lax.broadcasted_iota(jnp.int32, (1, nc), 1)
    s = jnp.where(n_row <= last_ok, s, NEG)
    m = jnp.max(s, axis=-1, keepdims=True)
    e = jnp.exp2(s - m)
    ov = _dot(e.astype(BF16), v_ref[...])
    inv = jnp.where(t_col >= CMP_BLOCK - 1, 1.0 / ov[:, HEAD_DIM:HEAD_DIM + 1], 0.0)
    o = (ov[:, 0:HEAD_DIM] * inv).astype(o_ref.dtype)
    for r in range(rows // tq):
        o_ref[:, r * HEAD_DIM:(r + 1) * HEAD_DIM] = o[r * tq:(r + 1) * tq]

    p = e * inv
    ps = p[0:tq]
    for r in range(1, rows // tq):
        ps = ps + p[r * tq:(r + 1) * tq]
    ps_hi, ps_lo = _split_bf16(ps)
    ovt = ovt_ref[...]
    imp = _dot_nt(ovt, ps_hi) + _dot_nt(ovt, ps_lo)
    n_sel = imp.shape[0]
    j = lax.broadcasted_iota(jnp.int32, (n_sel, tq), 0)
    cur = (t0 + lax.broadcasted_iota(jnp.int32, (n_sel, tq), 1)) // SEL_BLOCK
    forced = (j == 0) | (j == cur) | (j == cur - 1)
    score = jnp.where(forced, -2.0, jnp.where(j <= cur, imp, -1.0))
    jf = j.astype(F32)
    for _ in range(max(n_top - 3, 0)):
        mx = jnp.max(score, axis=0, keepdims=True)
        pick = jnp.min(jnp.where(score == mx, jf, float(n_sel)), axis=0, keepdims=True)
        score = jnp.where(jf == pick, -2.0, score)
    keep = (score == -2.0) & (j < t0 // SEL_BLOCK)
    bias_ref[...] = jnp.where(keep, 0.0, NEG).astype(BF16)


def _cmp_select(q5, kc, vca, ovt, tq, n_top):
    B, G, nt, rows, hd = q5.shape
    nc = kc.shape[2]
    n_sel = ovt.shape[0]
    gp = GROUP_PAIR
    return pl.pallas_call(
        functools.partial(_cmp_select_kernel, tq=tq, n_top=n_top),
        grid=(B, G // gp, nt),
        in_specs=[
            pl.BlockSpec((1, gp, 1, rows, hd), lambda b, g, i: (b, g, i, 0, 0)),
            pl.BlockSpec((1, gp, nc, hd), lambda b, g, i: (b, g, 0, 0)),
            pl.BlockSpec((1, gp, nc, LANES), lambda b, g, i: (b, g, 0, 0)),
            pl.BlockSpec(ovt.shape, lambda b, g, i: (0, 0)),
        ],
        out_specs=[
            pl.BlockSpec((tq, gp * rows // tq * hd), lambda b, g, i: (b * nt + i, g)),
            pl.BlockSpec((1, gp, 1, n_sel, tq), lambda b, g, i: (b, g, i, 0, 0)),
        ],
        out_shape=[
            jax.ShapeDtypeStruct((B * nt * tq, G * rows // tq * hd), BF16),
            jax.ShapeDtypeStruct((B, G, nt, n_sel, tq), BF16),
        ],
        compiler_params=_cparams("parallel", "parallel", "parallel"),
        name="cmp_select",
    )(q5, kc, vca, ovt)


def _sel_attn_kernel(q_ref, bias_ref, k_ref, v_ref, o_ref, s_ref, acc_ref, m_ref, *, tq, tk):
    i = pl.program_id(2)
    qt = q_ref[0, 0, 0]
    rows = qt.shape[1]
    reps = rows // tq
    t0 = pl.multiple_of(i * tq, tq)
    n_main = (t0 + tk - 1) // tk
    tiles_per_group = SEL_GROUP * SEL_BLOCK // tk

    def scores(kt):
        g0 = pl.multiple_of((kt // tiles_per_group) * SEL_GROUP, SEL_GROUP)
        b = bias_ref[0, 0, 0, pl.ds(g0, SEL_GROUP), :]
        qa = jnp.concatenate([qt, jnp.concatenate([b] * reps, axis=1)], axis=0)
        k = k_ref[0, 0, pl.ds(pl.multiple_of(kt * tk, tk), tk), :]
        return _dot(k, qa)

    def update(slot, kt):
        s = s_ref[slot]
        m = m_ref[...]
        m_new = jnp.maximum(m, jnp.max(s, axis=0, keepdims=True))
        p = jnp.exp2(s - m_new).astype(BF16)
        v = v_ref[0, 0, :, pl.ds(pl.multiple_of(kt * tk, tk), tk)]
        acc_ref[...] = jnp.exp2(m - m_new) * acc_ref[...] + _dot(v, p)
        m_ref[...] = m_new

    s = _dot(k_ref[0, 0, pl.ds(t0, tq), 0:HEAD_DIM], qt)
    kpos = t0 + lax.broadcasted_iota(jnp.int32, (tq, rows), 0)
    t_col = t0 + lax.broadcasted_iota(jnp.int32, (tq, rows), 1) % tq
    s = jnp.where(kpos <= t_col, s, NEG)
    m = jnp.max(s, axis=0, keepdims=True)
    m_ref[...] = m
    acc_ref[...] = _dot(v_ref[0, 0, :, pl.ds(t0, tq)], jnp.exp2(s - m).astype(BF16))

    s_ref[0] = scores(0)

    def body(jj, carry):
        k0 = 2 * jj
        s_ref[1] = scores(k0 + 1)
        update(0, k0)
        s_ref[0] = scores(jnp.minimum(k0 + 2, n_main - 1))
        update(1, k0 + 1)
        return carry

    lax.fori_loop(0, n_main // 2, body, 0)

    @pl.when(n_main % 2 == 1)
    def _():
        update(0, n_main - 1)

    o = acc_ref[0:HEAD_DIM, :] / acc_ref[HEAD_DIM:HEAD_DIM + 1, :]
    for r in range(reps):
        o_ref[:, r * HEAD_DIM:(r + 1) * HEAD_DIM] = o[:, r * tq:(r + 1) * tq].T.astype(o_ref.dtype)


def _sel_attn(qt, bias_t, ka, vat, tq, tk):
    B, G, nt, hd, rows = qt.shape
    S = ka.shape[2]
    nb = bias_t.shape[3]
    return pl.pallas_call(
        functools.partial(_sel_attn_kernel, tq=tq, tk=tk),
        grid=(B, G, nt),
        in_specs=[
            pl.BlockSpec((1, 1, 1, hd, rows), lambda b, g, i: (b, g, i, 0, 0)),
            pl.BlockSpec((1, 1, 1, nb, tq), lambda b, g, i: (b, g, i, 0, 0)),
            pl.BlockSpec((1, 1, S, LANES), lambda b, g, i: (b, g, 0, 0)),
            pl.BlockSpec((1, 1, PV_ROWS, S), lambda b, g, i: (b, g, 0, 0)),
        ],
        out_specs=pl.BlockSpec((tq, rows // tq * hd), lambda b, g, i: (b * nt + i, g)),
        out_shape=jax.ShapeDtypeStruct((B * nt * tq, G * rows // tq * hd), BF16),
        scratch_shapes=[pltpu.VMEM((2, tk, rows), F32),
                        pltpu.VMEM((PV_ROWS, rows), F32),
                        pltpu.VMEM((1, rows), F32)],
        compiler_params=_cparams("parallel", "parallel", "arbitrary"),
        name="sel_attn",
    )(qt, bias_t, ka, vat)


def _win_attn_kernel(q_ref, k_ref, v_ref, o_ref, *, tq):
    for gg in range(q_ref.shape[1]):
        _win_attn_group(q_ref.at[0, gg, 0], k_ref.at[0, gg], v_ref.at[0, gg],
                        o_ref.at[:, gg * NSA_HPG * HEAD_DIM:(gg + 1) * NSA_HPG * HEAD_DIM], tq=tq)


def _win_attn_group(q_ref, k_ref, v_ref, o_ref, *, tq):
    i = pl.program_id(2)
    q = q_ref[...]
    rows = q.shape[0]
    span = WINDOW + tq
    t0 = i * tq
    start = pl.multiple_of(jnp.maximum(t0 - WINDOW, 0), tq)
    k = k_ref[pl.ds(start, span), :]
    v = v_ref[pl.ds(start, span), :]
    s = _dot_nt(q, k)
    t_col = t0 + lax.broadcasted_iota(jnp.int32, (rows, 1), 0) % tq
    kpos = start + lax.broadcasted_iota(jnp.int32, (1, span), 1)
    dlt = lax.bitcast_convert_type(t_col - kpos, jnp.uint32)
    s = jnp.where(dlt < jnp.uint32(WINDOW), s, NEG)
    m = jnp.max(s, axis=-1, keepdims=True)
    ov = _dot(jnp.exp2(s - m).astype(BF16), v)
    o = (ov[:, 0:HEAD_DIM] / ov[:, HEAD_DIM:HEAD_DIM + 1]).astype(o_ref.dtype)
    for r in range(rows // tq):
        o_ref[:, r * HEAD_DIM:(r + 1) * HEAD_DIM] = o[r * tq:(r + 1) * tq]


def _win_attn(q5, kw, vwa, tq):
    B, G, nt, rows, hd = q5.shape
    S = kw.shape[2]
    gp = GROUP_PAIR
    return pl.pallas_call(
        functools.partial(_win_attn_kernel, tq=tq),
        grid=(B, G // gp, nt),
        in_specs=[
            pl.BlockSpec((1, gp, 1, rows, hd), lambda b, g, i: (b, g, i, 0, 0)),
            pl.BlockSpec((1, gp, S, hd), lambda b, g, i: (b, g, 0, 0)),
            pl.BlockSpec((1, gp, S, LANES), lambda b, g, i: (b, g, 0, 0)),
        ],
        out_specs=pl.BlockSpec((tq, gp * rows // tq * hd), lambda b, g, i: (b * nt + i, g)),
        out_shape=jax.ShapeDtypeStruct((B * nt * tq, G * rows // tq * hd), BF16),
        compiler_params=_cparams("parallel", "parallel", "arbitrary"),
        name="win_attn",
    )(q5, kw, vwa)


def _nsa_out_kernel(x_ref, oc_ref, os_ref, ow_ref, gate_ref, ex_ref, sz_ref, w_ref, y_ref):
    g_hi, g_lo = _split_bf16(gate_ref[...])
    o = None
    for c, o_ref in enumerate((oc_ref, os_ref, ow_ref)):
        e = ex_ref[c]
        term = (_dot(g_hi, e) + _dot(g_lo, e)) * o_ref[...].astype(F32)
        o = term if o is None else o + term
    u = (o * sz_ref[...].astype(F32)).astype(BF16)
    y_ref[...] = x_ref[...] + _dot(u, w_ref[...])


def _nsa_out(x2, oc, os_, ow, gates, expand, sz, w_out, tm):
    n, d = x2.shape
    row = lambda i: (i, 0)
    return pl.pallas_call(
        _nsa_out_kernel,
        grid=(n // tm,),
        in_specs=[
            pl.BlockSpec((tm, d), row),
            pl.BlockSpec((tm, d), row),
            pl.BlockSpec((tm, d), row),
            pl.BlockSpec((tm, d), row),
            pl.BlockSpec((tm, LANES), row),
            pl.BlockSpec(expand.shape, lambda i: (0, 0, 0)),
            pl.BlockSpec((tm, d), row),
            pl.BlockSpec(w_out.shape, lambda i: (0, 0)),
        ],
        out_specs=pl.BlockSpec((tm, d), row),
        out_shape=jax.ShapeDtypeStruct((n, d), F32),
        compiler_params=_cparams("parallel"),
        name="nsa_out",
    )(x2, oc, os_, ow, gates, expand, sz, w_out)


def _hg_proj_kernel(x_ref, g_ref, lb_ref, wq_ref, wf_ref, wi_ref, wz_ref,
                    q_ref, k_ref, lf_ref, v_ref, sz_ref):
    x = x_ref[...]
    h = x * lax.rsqrt(jnp.mean(x * x, axis=-1, keepdims=True) + EPS) * g_ref[...]
    hb = h.astype(BF16)
    qr = _dot(hb, wq_ref[...])
    q_ref[...] = (qr * _sigmoid(qr)).astype(BF16)
    lb = lb_ref[...]
    f = lb + (1.0 - lb) * _sigmoid(_dot(hb, wf_ref[...]))
    k_ref[...] = (1.0 - f).astype(BF16)
    lf_ref[...] = jnp.log(f)
    v_ref[...] = _dot(hb, wi_ref[...]).astype(BF16)
    z = _dot(hb, wz_ref[...])
    sz_ref[...] = (z * _sigmoid(z)).astype(BF16)


def _hg_proj(x2, g, lb, wq, wf, wi, wz, tm):
    n, d = x2.shape
    row = lambda i: (i, 0)
    fix = lambda i: (0, 0)
    return pl.pallas_call(
        _hg_proj_kernel,
        grid=(n // tm,),
        in_specs=[pl.BlockSpec((tm, d), row), pl.BlockSpec((1, d), fix), pl.BlockSpec((1, d), fix)]
        + [pl.BlockSpec((d, d), fix)] * 4,
        out_specs=[pl.BlockSpec((tm, d), row)] * 5,
        out_shape=[
            jax.ShapeDtypeStruct((n, d), BF16),
            jax.ShapeDtypeStruct((n, d), BF16),
            jax.ShapeDtypeStruct((n, d), F32),
            jax.ShapeDtypeStruct((n, d), BF16),
            jax.ShapeDtypeStruct((n, d), BF16),
        ],
        compiler_params=_cparams("parallel"),
        name="hg_proj",
    )(x2, g, lb, wq, wf, wi, wz)


def _hg_masks(c):
    t = np.arange(c)
    masks = []
    h = 1
    while h < c:
        same = (t[:, None] // (2 * h)) == (t[None, :] // (2 * h))
        upper = ((t[:, None] // h) % 2) == 1
        lower = ((t[None, :] // h) % 2) == 0
        masks.append((same & upper & lower).astype(np.float32))
        h *= 2
    masks.append(np.eye(c, dtype=np.float32))
    return np.stack(masks, axis=0)


def _hg_scan_kernel(q_ref, k_ref, lf_ref, v_ref, gn_ref, mk_ref, o_ref, st_ref, qd_ref, kd_ref, *, c):
    @pl.when(pl.program_id(2) == 0)
    def _():
        st_ref[...] = jnp.zeros_like(st_ref)

    ct = q_ref.shape[1]
    n_lvl = mk_ref.shape[0] - 1
    lf = lf_ref[0]
    pos = lax.broadcasted_iota(jnp.int32, lf.shape, 0) % c
    b = lf
    d = 1
    while d < c:
        b = b + jnp.where(pos >= d, pltpu.roll(b, d, 0), 0.0)
        d *= 2
    q = q_ref[0].astype(F32)
    k = k_ref[0].astype(F32)
    start, end = b - lf, b
    h = 1
    for lv in range(n_lvl + 1):
        qd_ref[lv] = (q * jnp.exp(b - start)).astype(BF16)
        kd_ref[lv] = (k * jnp.exp(end - b)).astype(BF16)
        if h < c:
            odd = (pos // h) % 2 == 1
            start = jnp.where(odd, pltpu.roll(start, h, 0), start)
            end = jnp.where(odd, end, pltpu.roll(end, ct - h, 0))
            h *= 2
    chunk_decay = jnp.exp(end)

    st = st_ref[...]
    for ci in range(ct // c):
        sl = slice(ci * c, (ci + 1) * c)
        v = v_ref[0, sl, :]
        att = mk_ref[n_lvl] * _dot_nt(q_ref[0, sl, :], k_ref[0, sl, :])
        for lv in range(n_lvl):
            att = att + mk_ref[lv] * _dot_nt(qd_ref[lv, sl, :], kd_ref[lv, sl, :])
        o = _dot_nt(qd_ref[n_lvl, sl, :], st.astype(BF16)) + _dot(att.astype(BF16), v)
        st = st * chunk_decay[ci * c:ci * c + 1, :] + _dot_tn(v, kd_ref[n_lvl, sl, :])
        on = o * lax.rsqrt(jnp.mean(o * o, axis=-1, keepdims=True) + EPS)
        o_ref[0, sl, :] = (on * gn_ref[...]).astype(o_ref.dtype)
    st_ref[...] = st


def _hg_scan(q, k, lf, v, gn, masks, ct):
    B, S, d = q.shape
    nh = d // HG_HEAD_DIM
    n_parts = masks.shape[0]
    blk = pl.BlockSpec((1, ct, HG_HEAD_DIM), lambda b, h, i: (b, i, h))
    return pl.pallas_call(
        functools.partial(_hg_scan_kernel, c=HG_CHUNK),
        grid=(B, nh, S // ct),
        in_specs=[blk, blk, blk, blk,
                  pl.BlockSpec((1, HG_HEAD_DIM), lambda b, h, i: (0, h)),
                  pl.BlockSpec(masks.shape, lambda b, h, i: (0, 0, 0))],
        out_specs=blk,
        out_shape=jax.ShapeDtypeStruct((B, S, d), BF16),
        scratch_shapes=[pltpu.VMEM((HG_HEAD_DIM, HG_HEAD_DIM), F32),
                        pltpu.VMEM((n_parts, ct, HG_HEAD_DIM), BF16),
                        pltpu.VMEM((n_parts, ct, HG_HEAD_DIM), BF16)],
        compiler_params=_cparams("parallel", "parallel", "arbitrary"),
        name="hg_scan",
    )(q, k, lf, v, gn, masks)


def _hg_out_kernel(x_ref, o_ref, sz_ref, w_ref, g_ref, y_ref):
    u = (o_ref[...].astype(F32) * sz_ref[...].astype(F32)).astype(BF16)
    x = x_ref[...] + _dot(u, w_ref[...])
    y_ref[...] = x * lax.rsqrt(jnp.mean(x * x, axis=-1, keepdims=True) + EPS) * g_ref[...]


def _hg_out(x2, o, sz, w_out, g, tm):
    n, d = x2.shape
    row = lambda i: (i, 0)
    fix = lambda i: (0, 0)
    return pl.pallas_call(
        _hg_out_kernel,
        grid=(n // tm,),
        in_specs=[pl.BlockSpec((tm, d), row), pl.BlockSpec((tm, d), row), pl.BlockSpec((tm, d), row),
                  pl.BlockSpec((d, d), fix), pl.BlockSpec((1, d), fix)],
        out_specs=pl.BlockSpec((tm, d), row),
        out_shape=jax.ShapeDtypeStruct((n, d), F32),
        compiler_params=_cparams("parallel"),
        name="hg_out",
    )(x2, o, sz, w_out, g)


def _nsa_layer(x2, B, S, g, w_in, pe_k, pe_v, wk1, wk2, wv1, wv2, w_out):
    H, G, R, hd = NSA_HEADS, NSA_GROUPS, NSA_HPG, HEAD_DIM
    n = B * S
    tm = 512
    tq = 128
    qw, kvw = H * hd, G * hd
    off = np.cumsum([0, qw] + [kvw] * 6 + [3 * H, qw])
    wq = w_in[:, off[0]:off[1]].astype(BF16)
    wkv = w_in[:, off[1]:off[7]].astype(BF16)
    wg = jnp.pad(w_in[:, off[7]:off[8]], ((0, 0), (0, LANES - 3 * H))).astype(BF16)
    wz = w_in[:, off[8]:off[9]].astype(BF16)
    pos = jnp.arange(S)
    cos, sin = _rope_tables(pos, LANES)
    q, kvc, ksw, vsw, gates, sz = _nsa_proj(x2, g[None, :], wq, wkv, wg, wz, cos, sin, S, tm)

    nc = S // CMP_STRIDE
    per = CMP_BLOCK // CMP_STRIDE
    assert per == 2
    def chunks(a):
        a = a.reshape(B, nc, CMP_STRIDE, G, hd).transpose(0, 3, 1, 2, 4)
        return a.reshape(B, G, nc, CMP_STRIDE * hd)
    cmp_end = jnp.arange(nc) * CMP_STRIDE + CMP_BLOCK - 1
    ccos, csin = _rope_tables(cmp_end, hd)
    swap = lambda w: jnp.concatenate([w[:, hd // 2:], w[:, :hd // 2]], axis=1)
    k_cmp = _compress(chunks(kvc[:, 0:kvw]), pe_k.reshape(per, CMP_STRIDE * hd), wk1.astype(BF16),
                      wk2.astype(BF16), swap(wk2).astype(BF16), ccos, csin)
    v_cmp = _compress(chunks(kvc[:, kvw:2 * kvw]), pe_v.reshape(per, CMP_STRIDE * hd), wv1.astype(BF16),
                      wv2.astype(BF16), swap(wv2).astype(BF16),
                      jnp.ones_like(ccos), jnp.zeros_like(csin))

    nt = S // tq
    q5 = q.reshape(B, nt, tq, G, R, hd).transpose(0, 3, 1, 4, 2, 5).reshape(B, G, nt, R * tq, hd)
    group_major = lambda a: a.reshape(B, S, G, hd).transpose(0, 2, 1, 3)
    ks, kw = group_major(ksw[:, 0:kvw]), group_major(ksw[:, kvw:2 * kvw])
    vs, vw = group_major(vsw[:, 0:kvw]), group_major(vsw[:, kvw:2 * kvw])

    n_sel = S // SEL_BLOCK
    n_top = min(SEL_TOPK, n_sel)
    cs = np.arange(nc)[:, None] * CMP_STRIDE
    ss = np.arange(n_sel)[None, :] * SEL_BLOCK
    overlap = ((cs < ss + SEL_BLOCK) & (cs + CMP_BLOCK > ss)).astype(np.float32)
    ovt = jnp.asarray(overlap.T, dtype=BF16)
    ones_col = jnp.zeros((1, LANES - hd), BF16).at[0, 0].set(1.0)
    with_ones = lambda v: jnp.concatenate(
        [v, jnp.broadcast_to(ones_col, v.shape[:-1] + (LANES - hd,))], axis=-1)
    o_c, bias_t = _cmp_select(q5, k_cmp, with_ones(v_cmp), ovt, tq, n_top)

    nsg = -(-n_sel // SEL_GROUP)
    bias_t = jnp.pad(bias_t, ((0, 0),) * 3 + ((0, nsg * SEL_GROUP - n_sel), (0, 0)), constant_values=NEG)
    qt = q.reshape(B, nt, tq, G, R, hd).transpose(0, 3, 1, 5, 4, 2).reshape(B, G, nt, hd, R * tq)
    onehot = jax.nn.one_hot((jnp.arange(S) // SEL_BLOCK) % SEL_GROUP, SEL_GROUP, dtype=BF16)
    ka = jnp.concatenate([ks, jnp.broadcast_to(onehot, (B, G, S, SEL_GROUP))], axis=-1)
    ones_row = jnp.zeros((PV_ROWS - hd, S), BF16).at[0, :].set(1.0)
    vat = jnp.concatenate([jnp.swapaxes(vs, 2, 3), jnp.broadcast_to(ones_row, (B, G, PV_ROWS - hd, S))], axis=2)
    tk = min(512, S)
    o_s = _sel_attn(qt, bias_t, ka, vat, tq, tk)
    o_w = _win_attn(q5, kw, with_ones(vw), tq)

    col = np.arange(3 * H)
    expand = np.zeros((3, LANES, qw), np.float32)
    for c in range(3):
        head = col[col % 3 == c] // 3
        for hh, cc in zip(head, col[col % 3 == c]):
            expand[c, cc, hh * hd:(hh + 1) * hd] = 1.0
    return _nsa_out(x2, o_c, o_s, o_w, gates,
                    jnp.asarray(expand, dtype=BF16), sz, w_out.astype(BF16), tm)


def _hgrn_layer(x2, B, S, g, w_in, lb, g_norm, w_out, final_g):
    n, d = x2.shape
    tm = 512
    wq, wf, wi, wz = (w_in[:, i * d:(i + 1) * d].astype(BF16) for i in range(4))
    q, k, lf, v, sz = _hg_proj(x2, g[None, :], lb[None, :], wq, wf, wi, wz, tm)
    r3 = lambda a: a.reshape(B, S, d)
    o = _hg_scan(r3(q), r3(k), r3(lf), r3(v), g_norm[None, :], jnp.asarray(_hg_masks(HG_CHUNK)),
                 min(512, S))
    return _hg_out(x2, o.reshape(n, d), sz, w_out.astype(BF16), final_g[None, :], tm)


def kernel(x, norm_w, nsa_w_in, nsa_pe_k, nsa_pe_v, nsa_wk1, nsa_wk2, nsa_wv1, nsa_wv2,
           nsa_w_out, hg_w_in, hg_lb_logits, hg_norm, hg_w_out, final_norm):
    B, S, d = x.shape
    depth = norm_w.shape[0]
    assert depth == 2 and nsa_w_in.shape[0] == 1 and hg_w_in.shape[0] == 1
    p = jax.nn.softmax(hg_lb_logits.astype(F32), axis=0)
    lower_bounds = jnp.cumsum(p, axis=0) - p[0]
    x2 = x.reshape(B * S, d)
    x2 = _nsa_layer(x2, B, S, norm_w[0], nsa_w_in[0], nsa_pe_k[0], nsa_pe_v[0], nsa_wk1[0],
                    nsa_wk2[0], nsa_wv1[0], nsa_wv2[0], nsa_w_out[0])
    out = _hgrn_layer(x2, B, S, norm_w[1], hg_w_in[0], lower_bounds[1], hg_norm[0], hg_w_out[0],
                      final_norm)
    return out.reshape(B, S, d)
```

```python
import functools

import jax
import jax.numpy as jnp
import numpy as np
from jax import lax
from jax.experimental import pallas as pl
from jax.experimental.pallas import tpu as pltpu

EPS = 1e-6
ROPE_THETA = 10000.0

NSA_HEADS = 16
NSA_GROUPS = 4
NSA_HPG = NSA_HEADS // NSA_GROUPS
HEAD_DIM = 64
CMP_BLOCK = 32
CMP_STRIDE = 16
SEL_BLOCK = 64
SEL_TOPK = 16
WINDOW = 512

HG_HEAD_DIM = 128
HG_CHUNK = 64

NEG = -1e30
LOG2E = 1.4426950408889634
LANES = 128
SEL_GROUP = 64
GROUP_PAIR = 2
CMP_BUCKETS = 4
PV_ROWS = HEAD_DIM + 16
VMEM_LIMIT = 52 * 1024 * 1024

F32 = jnp.float32
BF16 = jnp.bfloat16


def _cparams(*sem):
    return pltpu.CompilerParams(dimension_semantics=sem, vmem_limit_bytes=VMEM_LIMIT)


def _sigmoid(x):
    return 1.0 / (1.0 + jnp.exp(-x))


def _dot(a, b):
    return jnp.dot(a, b, preferred_element_type=F32)


def _dot_nt(a, b):
    return lax.dot_general(a, b, (((1,), (1,)), ((), ())), preferred_element_type=F32)


def _dot_tn(a, b):
    return lax.dot_general(a, b, (((0,), (0,)), ((), ())), preferred_element_type=F32)


def _split_bf16(x):
    hi = x.astype(BF16)
    lo = (x - hi.astype(F32)).astype(BF16)
    return hi, lo


def _rope_tables(pos, width):
    half = HEAD_DIM // 2
    inv = ROPE_THETA ** (-jnp.arange(half, dtype=F32) / half)
    ang = pos.astype(F32)[:, None] * inv[None, :]
    cos = jnp.concatenate([jnp.cos(ang), jnp.cos(ang)], axis=-1)
    sin = jnp.concatenate([-jnp.sin(ang), jnp.sin(ang)], axis=-1)
    reps = width // HEAD_DIM
    return jnp.tile(cos, (1, reps)), jnp.tile(sin, (1, reps))


def _rope128(x, cos, sin):
    lane = lax.broadcasted_iota(jnp.int32, x.shape, 1)
    first_half = (lane % HEAD_DIM) < (HEAD_DIM // 2)
    partner = jnp.where(first_half, pltpu.roll(x, LANES - HEAD_DIM // 2, 1),
                        pltpu.roll(x, HEAD_DIM // 2, 1))
    return x * cos + partner * sin


def _nsa_proj_kernel(x_ref, g_ref, wq_ref, wkv_ref, wg_ref, wz_ref, cos_ref, sin_ref, oh_ref, one_ref,
                     q_ref, q5_ref, kvc_ref, ka_ref, kw_ref, vs_ref, vwa_ref, gate_ref, sz_ref, *, tq):
    x = x_ref[...]
    h = x * lax.rsqrt(jnp.mean(x * x, axis=-1, keepdims=True) + EPS) * g_ref[...]
    hb = h.astype(BF16)
    cos = cos_ref[...]
    sin = sin_ref[...]
    hd, hpg = HEAD_DIM, NSA_HPG
    scale = HEAD_DIM ** -0.5 * LOG2E
    q = _dot(hb, wq_ref[...])
    for c in range(q.shape[1] // LANES):
        sl = slice(c * LANES, (c + 1) * LANES)
        slab = (_rope128(q[:, sl], cos, sin) * scale).astype(BF16)
        q_ref[:, sl] = slab
        for half in range(2):
            g, r = divmod(2 * c + half, hpg)
            for jq in range(slab.shape[0] // tq):
                q5_ref[0, g, jq, r * tq:(r + 1) * tq, :] = slab[jq * tq:(jq + 1) * tq, half * hd:(half + 1) * hd]
    kv = _dot(hb, wkv_ref[...])
    kvc_ref[...] = kv[:, 0:512]
    for c in range(2):
        ks = _rope128(kv[:, 512 + c * LANES:512 + (c + 1) * LANES], cos, sin).astype(BF16)
        kw = _rope128(kv[:, 1024 + c * LANES:1024 + (c + 1) * LANES], cos, sin).astype(BF16)
        for half in range(2):
            g = 2 * c + half
            lanes = slice(half * hd, (half + 1) * hd)
            ka_ref[0, g] = jnp.concatenate([ks[:, lanes], oh_ref[...]], axis=1)
            kw_ref[0, g] = kw[:, lanes]
    for g in range(NSA_GROUPS):
        vs_ref[0, g] = kv[:, 768 + g * hd:768 + (g + 1) * hd].astype(BF16)
        vwa_ref[0, g] = jnp.concatenate(
            [kv[:, 1280 + g * hd:1280 + (g + 1) * hd].astype(BF16), one_ref[...]], axis=1)
    gate_ref[...] = _sigmoid(_dot(hb, wg_ref[...]))
    z = _dot(hb, wz_ref[...])
    sz_ref[...] = (z * _sigmoid(z)).astype(BF16)


def _nsa_proj(x2, g, wq, wkv, wg, wz, cos, sin, onehot, ones_tile, S, tm, tq):
    n, d = x2.shape
    nt_s = S // tm
    B, G, R, hd = n // S, NSA_GROUPS, NSA_HPG, HEAD_DIM
    row = lambda i: (i, 0)
    fix = lambda i: (0, 0)
    pos = lambda i: (i % nt_s, 0)
    grp = lambda i: (i // nt_s, 0, i % nt_s, 0)
    return pl.pallas_call(
        functools.partial(_nsa_proj_kernel, tq=tq),
        grid=(n // tm,),
        in_specs=[
            pl.BlockSpec((tm, d), row),
            pl.BlockSpec((1, d), fix),
            pl.BlockSpec(wq.shape, fix),
            pl.BlockSpec(wkv.shape, fix),
            pl.BlockSpec(wg.shape, fix),
            pl.BlockSpec(wz.shape, fix),
            pl.BlockSpec((tm, LANES), pos),
            pl.BlockSpec((tm, LANES), pos),
            pl.BlockSpec((tm, SEL_GROUP), pos),
            pl.BlockSpec((tm, LANES - hd), fix),
        ],
        out_specs=[
            pl.BlockSpec((tm, 1024), row),
            pl.BlockSpec((1, G, tm // tq, R * tq, hd), lambda i: (i // nt_s, 0, i % nt_s, 0, 0)),
            pl.BlockSpec((tm, 512), row),
            pl.BlockSpec((1, G, tm, LANES), grp),
            pl.BlockSpec((1, G, tm, hd), grp),
            pl.BlockSpec((1, G, tm, hd), grp),
            pl.BlockSpec((1, G, tm, LANES), grp),
            pl.BlockSpec((tm, LANES), row),
            pl.BlockSpec((tm, 1024), row),
        ],
        out_shape=[
            jax.ShapeDtypeStruct((n, 1024), BF16),
            jax.ShapeDtypeStruct((B, G, S // tq, R * tq, hd), BF16),
            jax.ShapeDtypeStruct((n, 512), F32),
            jax.ShapeDtypeStruct((B, G, S, LANES), BF16),
            jax.ShapeDtypeStruct((B, G, S, hd), BF16),
            jax.ShapeDtypeStruct((B, G, S, hd), BF16),
            jax.ShapeDtypeStruct((B, G, S, LANES), BF16),
            jax.ShapeDtypeStruct((n, LANES), F32),
            jax.ShapeDtypeStruct((n, 1024), BF16),
        ],
        compiler_params=_cparams("parallel"),
        name="nsa_proj",
    )(x2, g, wq, wkv, wg, wz, cos, sin, onehot, ones_tile)


def _compress_kernel(a_ref, pe_ref, w1_ref, w2_ref, w2p_ref, cos_ref, sin_ref, o_ref):
    a = a_ref[0, 0]
    nc = a.shape[0]
    half = a.shape[1]
    lo = _dot((a + pe_ref[0:1, :]).astype(BF16), w1_ref[0:half, :])
    hi = _dot((a + pe_ref[1:2, :]).astype(BF16), w1_ref[half:2 * half, :])
    hid = lo + pltpu.roll(hi, nc - 1, 0)
    act = (hid * _sigmoid(hid)).astype(BF16)
    raw = _dot(act, w2_ref[...])
    partner = _dot(act, w2p_ref[...])
    o_ref[0, 0] = (raw * cos_ref[...] + partner * sin_ref[...]).astype(BF16)


def _compress(a, pe2, w1, w2, w2p, cos, sin):
    B, G, nc, f = a.shape
    fix = lambda b, g: (0, 0)
    return pl.pallas_call(
        _compress_kernel,
        grid=(B, G),
        in_specs=[
            pl.BlockSpec((1, 1, nc, f), lambda b, g: (b, g, 0, 0)),
            pl.BlockSpec(pe2.shape, fix),
            pl.BlockSpec(w1.shape, fix),
            pl.BlockSpec(w2.shape, fix),
            pl.BlockSpec(w2p.shape, fix),
            pl.BlockSpec(cos.shape, fix),
            pl.BlockSpec(sin.shape, fix),
        ],
        out_specs=pl.BlockSpec((1, 1, nc, HEAD_DIM), lambda b, g: (b, g, 0, 0)),
        out_shape=jax.ShapeDtypeStruct((B, G, nc, HEAD_DIM), BF16),
        compiler_params=_cparams("parallel", "parallel"),
        name="compress",
    )(a, pe2, w1, w2, w2p, cos, sin)


def _cmp_select_kernel(q_ref, k_ref, v_ref, ovt_ref, o_ref, bias_ref, *, tq, n_top):
    nc_all = k_ref.shape[2]
    n_bkt = CMP_BUCKETS if nc_all % (CMP_BUCKETS * LANES) == 0 else 1
    need = (pl.program_id(2) * tq + tq - CMP_BLOCK) // CMP_STRIDE + 1
    for bk in range(n_bkt):
        lo, nk = nc_all * bk // n_bkt, nc_all * (bk + 1) // n_bkt
        cond = need > lo if bk == n_bkt - 1 else (need <= nk if bk == 0 else (need > lo) & (need <= nk))

        @pl.when(cond)
        def _(nk=nk):
            for gg in range(q_ref.shape[1]):
                _cmp_select_group(q_ref.at[0, gg, 0], k_ref.at[0, gg, 0:nk], v_ref.at[0, gg, 0:nk],
                                  ovt_ref.at[:, 0:nk],
                                  o_ref.at[:, gg * NSA_HPG * HEAD_DIM:(gg + 1) * NSA_HPG * HEAD_DIM],
                                  bias_ref.at[0, gg, 0], tq=tq, n_top=n_top)


def _cmp_select_group(q_ref, k_ref, v_ref, ovt_ref, o_ref, bias_ref, *, tq, n_top):
    i = pl.program_id(2)
    q = q_ref[...]
    rows = q.shape[0]
    nc = k_ref.shape[0]
    t0 = i * tq
    s = _dot_nt(q, k_ref[...])
    t_col = t0 + lax.broadcasted_iota(jnp.int32, (rows, 1), 0) % tq
    last_ok = (t_col - (CMP_BLOCK - 1)) // CMP_STRIDE
    n_row = lax.broadcasted_iota(jnp.int32, (1, nc), 1)
    s = jnp.where(n_row <= last_ok, s, NEG)
    m = jnp.max(s, axis=-1, keepdims=True)
    e = jnp.exp2(s - m)
    ov = _dot(e.astype(BF16), v_ref[...])
    inv = jnp.where(t_col >= CMP_BLOCK - 1, 1.0 / ov[:, HEAD_DIM:HEAD_DIM + 1], 0.0)
    o = (ov[:, 0:HEAD_DIM] * inv).astype(o_ref.dtype)
    for r in range(rows // tq):
        o_ref[:, r * HEAD_DIM:(r + 1) * HEAD_DIM] = o[r * tq:(r + 1) * tq]

    p = e * inv
    ps = p[0:tq]
    for r in range(1, rows // tq):
        ps = ps + p[r * tq:(r + 1) * tq]
    ps_hi, ps_lo = _split_bf16(ps)
    ovt = ovt_ref[...]
    imp = _dot_nt(ovt, ps_hi) + _dot_nt(ovt, ps_lo)
    n_sel = imp.shape[0]
    j = lax.broadcasted_iota(jnp.int32, (n_sel, tq), 0)
    cur = (t0 + lax.broadcasted_iota(jnp.int32, (n_sel, tq), 1)) // SEL_BLOCK
    forced = (j == 0) | (j == cur) | (j == cur - 1)
    score = jnp.where(forced, -2.0, jnp.where(j <= cur, imp, -1.0))
    jf = j.astype(F32)
    for _ in range(max(n_top - 3, 0)):
        mx = jnp.max(score, axis=0, keepdims=True)
        pick = jnp.min(jnp.where(score == mx, jf, float(n_sel)), axis=0, keepdims=True)
        score = jnp.where(jf == pick, -2.0, score)
    keep = (score == -2.0) & (j < t0 // SEL_BLOCK)
    bias_ref[...] = jnp.where(keep, 0.0, NEG).astype(BF16)


def _cmp_select(q5, kc, vca, ovt, tq, n_top):
    B, G, nt, rows, hd = q5.shape
    nc = kc.shape[2]
    n_sel = ovt.shape[0]
    gp = GROUP_PAIR
    return pl.pallas_call(
        functools.partial(_cmp_select_kernel, tq=tq, n_top=n_top),
        grid=(B, G // gp, nt),
        in_specs=[
            pl.BlockSpec((1, gp, 1, rows, hd), lambda b, g, i: (b, g, i, 0, 0)),
            pl.BlockSpec((1, gp, nc, hd), lambda b, g, i: (b, g, 0, 0)),
            pl.BlockSpec((1, gp, nc, LANES), lambda b, g, i: (b, g, 0, 0)),
            pl.BlockSpec(ovt.shape, lambda b, g, i: (0, 0)),
        ],
        out_specs=[
            pl.BlockSpec((tq, gp * rows // tq * hd), lambda b, g, i: (b * nt + i, g)),
            pl.BlockSpec((1, gp, 1, n_sel, tq), lambda b, g, i: (b, g, i, 0, 0)),
        ],
        out_shape=[
            jax.ShapeDtypeStruct((B * nt * tq, G * rows // tq * hd), BF16),
            jax.ShapeDtypeStruct((B, G, nt, n_sel, tq), BF16),
        ],
        compiler_params=_cparams("parallel", "parallel", "parallel"),
        name="cmp_select",
    )(q5, kc, vca, ovt)


def _sel_attn_kernel(q_ref, bias_ref, k_ref, v_ref, o_ref, s_ref, acc_ref, m_ref, *, tq, tk):
    i = pl.program_id(2)
    qt = q_ref[0, 0, 0]
    rows = qt.shape[1]
    reps = rows // tq
    t0 = pl.multiple_of(i * tq, tq)
    n_main = (t0 + tk - 1) // tk
    tiles_per_group = SEL_GROUP * SEL_BLOCK // tk

    def scores(kt):
        g0 = pl.multiple_of((kt // tiles_per_group) * SEL_GROUP, SEL_GROUP)
        b = bias_ref[0, 0, 0, pl.ds(g0, SEL_GROUP), :]
        qa = jnp.concatenate([qt, jnp.concatenate([b] * reps, axis=1)], axis=0)
        k = k_ref[0, 0, pl.ds(pl.multiple_of(kt * tk, tk), tk), :]
        return _dot(k, qa)

    def update(slot, kt):
        s = s_ref[slot]
        m = m_ref[...]
        m_new = jnp.maximum(m, jnp.max(s, axis=0, keepdims=True))
        p = jnp.exp2(s - m_new).astype(BF16)
        v = v_ref[0, 0, :, pl.ds(pl.multiple_of(kt * tk, tk), tk)]
        acc_ref[...] = jnp.exp2(m - m_new) * acc_ref[...] + _dot(v, p)
        m_ref[...] = m_new

    s = _dot(k_ref[0, 0, pl.ds(t0, tq), 0:HEAD_DIM], qt)
    kpos = t0 + lax.broadcasted_iota(jnp.int32, (tq, rows), 0)
    t_col = t0 + lax.broadcasted_iota(jnp.int32, (tq, rows), 1) % tq
    s = jnp.where(kpos <= t_col, s, NEG)
    m = jnp.max(s, axis=0, keepdims=True)
    m_ref[...] = m
    acc_ref[...] = _dot(v_ref[0, 0, :, pl.ds(t0, tq)], jnp.exp2(s - m).astype(BF16))

    s_ref[0] = scores(0)

    def body(jj, carry):
        k0 = 2 * jj
        s_ref[1] = scores(k0 + 1)
        update(0, k0)
        s_ref[0] = scores(jnp.minimum(k0 + 2, n_main - 1))
        update(1, k0 + 1)
        return carry

    lax.fori_loop(0, n_main // 2, body, 0)

    @pl.when(n_main % 2 == 1)
    def _():
        update(0, n_main - 1)

    o = acc_ref[0:HEAD_DIM, :] / acc_ref[HEAD_DIM:HEAD_DIM + 1, :]
    for r in range(reps):
        o_ref[:, r * HEAD_DIM:(r + 1) * HEAD_DIM] = o[:, r * tq:(r + 1) * tq].T.astype(o_ref.dtype)


def _sel_attn(qt, bias_t, ka, vat, tq, tk):
    B, G, nt, hd, rows = qt.shape
    S = ka.shape[2]
    nb = bias_t.shape[3]
    return pl.pallas_call(
        functools.partial(_sel_attn_kernel, tq=tq, tk=tk),
        grid=(B, G, nt),
        in_specs=[
            pl.BlockSpec((1, 1, 1, hd, rows), lambda b, g, i: (b, g, i, 0, 0)),
            pl.BlockSpec((1, 1, 1, nb, tq), lambda b, g, i: (b, g, i, 0, 0)),
            pl.BlockSpec((1, 1, S, LANES), lambda b, g, i: (b, g, 0, 0)),
            pl.BlockSpec((1, 1, PV_ROWS, S), lambda b, g, i: (b, g, 0, 0)),
        ],
        out_specs=pl.BlockSpec((tq, rows // tq * hd), lambda b, g, i: (b * nt + i, g)),
        out_shape=jax.ShapeDtypeStruct((B * nt * tq, G * rows // tq * hd), BF16),
        scratch_shapes=[pltpu.VMEM((2, tk, rows), F32),
                        pltpu.VMEM((PV_ROWS, rows), F32),
                        pltpu.VMEM((1, rows), F32)],
        compiler_params=_cparams("parallel", "parallel", "arbitrary"),
        name="sel_attn",
    )(qt, bias_t, ka, vat)


def _win_attn_kernel(q_ref, k_ref, v_ref, o_ref, *, tq):
    for gg in range(q_ref.shape[1]):
        _win_attn_group(q_ref.at[0, gg, 0], k_ref.at[0, gg], v_ref.at[0, gg],
                        o_ref.at[:, gg * NSA_HPG * HEAD_DIM:(gg + 1) * NSA_HPG * HEAD_DIM], tq=tq)


def _win_attn_group(q_ref, k_ref, v_ref, o_ref, *, tq):
    i = pl.program_id(2)
    q = q_ref[...]
    rows = q.shape[0]
    span = WINDOW + tq
    t0 = i * tq
    start = pl.multiple_of(jnp.maximum(t0 - WINDOW, 0), tq)
    k = k_ref[pl.ds(start, span), :]
    v = v_ref[pl.ds(start, span), :]
    s = _dot_nt(q, k)
    t_col = t0 + lax.broadcasted_iota(jnp.int32, (rows, 1), 0) % tq
    kpos = start + lax.broadcasted_iota(jnp.int32, (1, span), 1)
    dlt = lax.bitcast_convert_type(t_col - kpos, jnp.uint32)
    s = jnp.where(dlt < jnp.uint32(WINDOW), s, NEG)
    m = jnp.max(s, axis=-1, keepdims=True)
    ov = _dot(jnp.exp2(s - m).astype(BF16), v)
    o = (ov[:, 0:HEAD_DIM] / ov[:, HEAD_DIM:HEAD_DIM + 1]).astype(o_ref.dtype)
    for r in range(rows // tq):
        o_ref[:, r * HEAD_DIM:(r + 1) * HEAD_DIM] = o[r * tq:(r + 1) * tq]


def _win_attn(q5, kw, vwa, tq):
    B, G, nt, rows, hd = q5.shape
    S = kw.shape[2]
    gp = GROUP_PAIR
    return pl.pallas_call(
        functools.partial(_win_attn_kernel, tq=tq),
        grid=(B, G // gp, nt),
        in_specs=[
            pl.BlockSpec((1, gp, 1, rows, hd), lambda b, g, i: (b, g, i, 0, 0)),
            pl.BlockSpec((1, gp, S, hd), lambda b, g, i: (b, g, 0, 0)),
            pl.BlockSpec((1, gp, S, LANES), lambda b, g, i: (b, g, 0, 0)),
        ],
        out_specs=pl.BlockSpec((tq, gp * rows // tq * hd), lambda b, g, i: (b * nt + i, g)),
        out_shape=jax.ShapeDtypeStruct((B * nt * tq, G * rows // tq * hd), BF16),
        compiler_params=_cparams("parallel", "parallel", "arbitrary"),
        name="win_attn",
    )(q5, kw, vwa)


def _nsa_out_kernel(x_ref, oc_ref, os_ref, ow_ref, gate_ref, ex_ref, sz_ref, w_ref, y_ref):
    g_hi, g_lo = _split_bf16(gate_ref[...])
    o = None
    for c, o_ref in enumerate((oc_ref, os_ref, ow_ref)):
        e = ex_ref[c]
        term = (_dot(g_hi, e) + _dot(g_lo, e)) * o_ref[...].astype(F32)
        o = term if o is None else o + term
    u = (o * sz_ref[...].astype(F32)).astype(BF16)
    y_ref[...] = x_ref[...] + _dot(u, w_ref[...])


def _nsa_out(x2, oc, os_, ow, gates, expand, sz, w_out, tm):
    n, d = x2.shape
    row = lambda i: (i, 0)
    return pl.pallas_call(
        _nsa_out_kernel,
        grid=(n // tm,),
        in_specs=[
            pl.BlockSpec((tm, d), row),
            pl.BlockSpec((tm, d), row),
            pl.BlockSpec((tm, d), row),
            pl.BlockSpec((tm, d), row),
            pl.BlockSpec((tm, LANES), row),
            pl.BlockSpec(expand.shape, lambda i: (0, 0, 0)),
            pl.BlockSpec((tm, d), row),
            pl.BlockSpec(w_out.shape, lambda i: (0, 0)),
        ],
        out_specs=pl.BlockSpec((tm, d), row),
        out_shape=jax.ShapeDtypeStruct((n, d), F32),
        compiler_params=_cparams("parallel"),
        name="nsa_out",
    )(x2, oc, os_, ow, gates, expand, sz, w_out)


def _hg_proj_kernel(x_ref, g_ref, lb_ref, wq_ref, wf_ref, wi_ref, wz_ref,
                    q_ref, k_ref, lf_ref, v_ref, sz_ref):
    x = x_ref[...]
    h = x * lax.rsqrt(jnp.mean(x * x, axis=-1, keepdims=True) + EPS) * g_ref[...]
    hb = h.astype(BF16)
    qr = _dot(hb, wq_ref[...])
    q_ref[...] = (qr * _sigmoid(qr)).astype(BF16)
    lb = lb_ref[...]
    f = lb + (1.0 - lb) * _sigmoid(_dot(hb, wf_ref[...]))
    k_ref[...] = (1.0 - f).astype(BF16)
    lf_ref[...] = jnp.log(f)
    v_ref[...] = _dot(hb, wi_ref[...]).astype(BF16)
    z = _dot(hb, wz_ref[...])
    sz_ref[...] = (z * _sigmoid(z)).astype(BF16)


def _hg_proj(x2, g, lb, wq, wf, wi, wz, tm):
    n, d = x2.shape
    row = lambda i: (i, 0)
    fix = lambda i: (0, 0)
    return pl.pallas_call(
        _hg_proj_kernel,
        grid=(n // tm,),
        in_specs=[pl.BlockSpec((tm, d), row), pl.BlockSpec((1, d), fix), pl.BlockSpec((1, d), fix)]
        + [pl.BlockSpec((d, d), fix)] * 4,
        out_specs=[pl.BlockSpec((tm, d), row)] * 5,
        out_shape=[
            jax.ShapeDtypeStruct((n, d), BF16),
            jax.ShapeDtypeStruct((n, d), BF16),
            jax.ShapeDtypeStruct((n, d), F32),
            jax.ShapeDtypeStruct((n, d), BF16),
            jax.ShapeDtypeStruct((n, d), BF16),
        ],
        compiler_params=_cparams("parallel"),
        name="hg_proj",
    )(x2, g, lb, wq, wf, wi, wz)


def _hg_masks(c):
    t = np.arange(c)
    masks = []
    h = 1
    while h < c:
        same = (t[:, None] // (2 * h)) == (t[None, :] // (2 * h))
        upper = ((t[:, None] // h) % 2) == 1
        lower = ((t[None, :] // h) % 2) == 0
        masks.append((same & upper & lower).astype(np.float32))
        h *= 2
    masks.append(np.eye(c, dtype=np.float32))
    return np.stack(masks, axis=0)


def _hg_scan_kernel(q_ref, k_ref, lf_ref, v_ref, gn_ref, mk_ref, o_ref, st_ref, qd_ref, kd_ref, *, c):
    @pl.when(pl.program_id(2) == 0)
    def _():
        st_ref[...] = jnp.zeros_like(st_ref)

    ct = q_ref.shape[1]
    n_lvl = mk_ref.shape[0] - 1
    lf = lf_ref[0]
    pos = lax.broadcasted_iota(jnp.int32, lf.shape, 0) % c
    b = lf
    d = 1
    while d < c:
        b = b + jnp.where(pos >= d, pltpu.roll(b, d, 0), 0.0)
        d *= 2
    q = q_ref[0].astype(F32)
    k = k_ref[0].astype(F32)
    start, end = b - lf, b
    h = 1
    for lv in range(n_lvl + 1):
        qd_ref[lv] = (q * jnp.exp(b - start)).astype(BF16)
        kd_ref[lv] = (k * jnp.exp(end - b)).astype(BF16)
        if h < c:
            odd = (pos // h) % 2 == 1
            start = jnp.where(odd, pltpu.roll(start, h, 0), start)
            end = jnp.where(odd, end, pltpu.roll(end, ct - h, 0))
            h *= 2
    chunk_decay = jnp.exp(end)

    st = st_ref[...]
    for ci in range(ct // c):
        sl = slice(ci * c, (ci + 1) * c)
        v = v_ref[0, sl, :]
        att = mk_ref[n_lvl] * _dot_nt(q_ref[0, sl, :], k_ref[0, sl, :])
        for lv in range(n_lvl):
            att = att + mk_ref[lv] * _dot_nt(qd_ref[lv, sl, :], kd_ref[lv, sl, :])
        o = _dot_nt(qd_ref[n_lvl, sl, :], st.astype(BF16)) + _dot(att.astype(BF16), v)
        st = st * chunk_decay[ci * c:ci * c + 1, :] + _dot_tn(v, kd_ref[n_lvl, sl, :])
        on = o * lax.rsqrt(jnp.mean(o * o, axis=-1, keepdims=True) + EPS)
        o_ref[0, sl, :] = (on * gn_ref[...]).astype(o_ref.dtype)
    st_ref[...] = st


def _hg_scan(q, k, lf, v, gn, masks, ct):
    B, S, d = q.shape
    nh = d // HG_HEAD_DIM
    n_parts = masks.shape[0]
    blk = pl.BlockSpec((1, ct, HG_HEAD_DIM), lambda b, h, i: (b, i, h))
    return pl.pallas_call(
        functools.partial(_hg_scan_kernel, c=HG_CHUNK),
        grid=(B, nh, S // ct),
        in_specs=[blk, blk, blk, blk,
                  pl.BlockSpec((1, HG_HEAD_DIM), lambda b, h, i: (0, h)),
                  pl.BlockSpec(masks.shape, lambda b, h, i: (0, 0, 0))],
        out_specs=blk,
        out_shape=jax.ShapeDtypeStruct((B, S, d), BF16),
        scratch_shapes=[pltpu.VMEM((HG_HEAD_DIM, HG_HEAD_DIM), F32),
                        pltpu.VMEM((n_parts, ct, HG_HEAD_DIM), BF16),
                        pltpu.VMEM((n_parts, ct, HG_HEAD_DIM), BF16)],
        compiler_params=_cparams("parallel", "parallel", "arbitrary"),
        name="hg_scan",
    )(q, k, lf, v, gn, masks)


def _hg_out_kernel(x_ref, o_ref, sz_ref, w_ref, g_ref, y_ref):
    u = (o_ref[...].astype(F32) * sz_ref[...].astype(F32)).astype(BF16)
    x = x_ref[...] + _dot(u, w_ref[...])
    y_ref[...] = x * lax.rsqrt(jnp.mean(x * x, axis=-1, keepdims=True) + EPS) * g_ref[...]


def _hg_out(x2, o, sz, w_out, g, tm):
    n, d = x2.shape
    row = lambda i: (i, 0)
    fix = lambda i: (0, 0)
    return pl.pallas_call(
        _hg_out_kernel,
        grid=(n // tm,),
        in_specs=[pl.BlockSpec((tm, d), row), pl.BlockSpec((tm, d), row), pl.BlockSpec((tm, d), row),
                  pl.BlockSpec((d, d), fix), pl.BlockSpec((1, d), fix)],
        out_specs=pl.BlockSpec((tm, d), row),
        out_shape=jax.ShapeDtypeStruct((n, d), F32),
        compiler_params=_cparams("parallel"),
        name="hg_out",
    )(x2, o, sz, w_out, g)


def _nsa_layer(x2, B, S, g, w_in, pe_k, pe_v, wk1, wk2, wv1, wv2, w_out):
    H, G, R, hd = NSA_HEADS, NSA_GROUPS, NSA_HPG, HEAD_DIM
    n = B * S
    tm = 512
    tq = 128
    qw, kvw = H * hd, G * hd
    off = np.cumsum([0, qw] + [kvw] * 6 + [3 * H, qw])
    wq = w_in[:, off[0]:off[1]].astype(BF16)
    wkv = w_in[:, off[1]:off[7]].astype(BF16)
    wg = jnp.pad(w_in[:, off[7]:off[8]], ((0, 0), (0, LANES - 3 * H))).astype(BF16)
    wz = w_in[:, off[8]:off[9]].astype(BF16)
    pos = jnp.arange(S)
    cos, sin = _rope_tables(pos, LANES)
    onehot = jax.nn.one_hot((pos // SEL_BLOCK) % SEL_GROUP, SEL_GROUP, dtype=BF16)
    ones_col = jnp.zeros((1, LANES - hd), BF16).at[0, 0].set(1.0)
    q, q5, kvc, ka, kw, vs, vwa, gates, sz = _nsa_proj(
        x2, g[None, :], wq, wkv, wg, wz, cos, sin, onehot, jnp.broadcast_to(ones_col, (tm, LANES - hd)),
        S, tm, tq)

    nc = S // CMP_STRIDE
    per = CMP_BLOCK // CMP_STRIDE
    assert per == 2
    def chunks(a):
        a = a.reshape(B, nc, CMP_STRIDE, G, hd).transpose(0, 3, 1, 2, 4)
        return a.reshape(B, G, nc, CMP_STRIDE * hd)
    cmp_end = jnp.arange(nc) * CMP_STRIDE + CMP_BLOCK - 1
    ccos, csin = _rope_tables(cmp_end, hd)
    swap = lambda w: jnp.concatenate([w[:, hd // 2:], w[:, :hd // 2]], axis=1)
    k_cmp = _compress(chunks(kvc[:, 0:kvw]), pe_k.reshape(per, CMP_STRIDE * hd), wk1.astype(BF16),
                      wk2.astype(BF16), swap(wk2).astype(BF16), ccos, csin)
    v_cmp = _compress(chunks(kvc[:, kvw:2 * kvw]), pe_v.reshape(per, CMP_STRIDE * hd), wv1.astype(BF16),
                      wv2.astype(BF16), swap(wv2).astype(BF16),
                      jnp.ones_like(ccos), jnp.zeros_like(csin))

    nt = S // tq

    n_sel = S // SEL_BLOCK
    n_top = min(SEL_TOPK, n_sel)
    cs = np.arange(nc)[:, None] * CMP_STRIDE
    ss = np.arange(n_sel)[None, :] * SEL_BLOCK
    overlap = ((cs < ss + SEL_BLOCK) & (cs + CMP_BLOCK > ss)).astype(np.float32)
    ovt = jnp.asarray(overlap.T, dtype=BF16)
    with_ones = lambda v: jnp.concatenate(
        [v, jnp.broadcast_to(ones_col, v.shape[:-1] + (LANES - hd,))], axis=-1)
    o_c, bias_t = _cmp_select(q5, k_cmp, with_ones(v_cmp), ovt, tq, n_top)

    nsg = -(-n_sel // SEL_GROUP)
    bias_t = jnp.pad(bias_t, ((0, 0),) * 3 + ((0, nsg * SEL_GROUP - n_sel), (0, 0)), constant_values=NEG)
    qt = q.reshape(B, nt, tq, G, R, hd).transpose(0, 3, 1, 5, 4, 2).reshape(B, G, nt, hd, R * tq)
    ones_row = jnp.zeros((PV_ROWS - hd, S), BF16).at[0, :].set(1.0)
    vat = jnp.concatenate([jnp.swapaxes(vs, 2, 3), jnp.broadcast_to(ones_row, (B, G, PV_ROWS - hd, S))], axis=2)
    tk = min(512, S)
    o_s = _sel_attn(qt, bias_t, ka, vat, tq, tk)
    o_w = _win_attn(q5, kw, vwa, tq)

    col = np.arange(3 * H)
    expand = np.zeros((3, LANES, qw), np.float32)
    for c in range(3):
        head = col[col % 3 == c] // 3
        for hh, cc in zip(head, col[col % 3 == c]):
            expand[c, cc, hh * hd:(hh + 1) * hd] = 1.0
    return _nsa_out(x2, o_c, o_s, o_w, gates,
                    jnp.asarray(expand, dtype=BF16), sz, w_out.astype(BF16), tm)


def _hgrn_layer(x2, B, S, g, w_in, lb, g_norm, w_out, final_g):
    n, d = x2.shape
    tm = 512
    wq, wf, wi, wz = (w_in[:, i * d:(i + 1) * d].astype(BF16) for i in range(4))
    q, k, lf, v, sz = _hg_proj(x2, g[None, :], lb[None, :], wq, wf, wi, wz, tm)
    r3 = lambda a: a.reshape(B, S, d)
    o = _hg_scan(r3(q), r3(k), r3(lf), r3(v), g_norm[None, :], jnp.asarray(_hg_masks(HG_CHUNK)),
                 min(512, S))
    return _hg_out(x2, o.reshape(n, d), sz, w_out.astype(BF16), final_g[None, :], tm)


def kernel(x, norm_w, nsa_w_in, nsa_pe_k, nsa_pe_v, nsa_wk1, nsa_wk2, nsa_wv1, nsa_wv2,
           nsa_w_out, hg_w_in, hg_lb_logits, hg_norm, hg_w_out, final_norm):
    B, S, d = x.shape
    depth = norm_w.shape[0]
    assert depth == 2 and nsa_w_in.shape[0] == 1 and hg_w_in.shape[0] == 1
    p = jax.nn.softmax(hg_lb_logits.astype(F32), axis=0)
    lower_bounds = jnp.cumsum(p, axis=0) - p[0]
    x2 = x.reshape(B * S, d)
    x2 = _nsa_layer(x2, B, S, norm_w[0], nsa_w_in[0], nsa_pe_k[0], nsa_pe_v[0], nsa_wk1[0],
                    nsa_wk2[0], nsa_wv1[0], nsa_wv2[0], nsa_w_out[0])
    out = _hgrn_layer(x2, B, S, norm_w[1], hg_w_in[0], lower_bounds[1], hg_norm[0], hg_w_out[0],
                      final_norm)
    return out.reshape(B, S, d)
```

```python
import functools

import jax
import jax.numpy as jnp
import numpy as np
from jax import lax
from jax.experimental import pallas as pl
from jax.experimental.pallas import tpu as pltpu

EPS = 1e-6
ROPE_THETA = 10000.0

NSA_HEADS = 16
NSA_GROUPS = 4
NSA_HPG = NSA_HEADS // NSA_GROUPS
HEAD_DIM = 64
CMP_BLOCK = 32
CMP_STRIDE = 16
SEL_BLOCK = 64
SEL_TOPK = 16
WINDOW = 512

HG_HEAD_DIM = 128
HG_CHUNK = 64

NEG = -1e30
LOG2E = 1.4426950408889634
LANES = 128
SEL_GROUP = 64
GROUP_PAIR = 2
CMP_BUCKETS = 4
PV_ROWS = HEAD_DIM + 16
VMEM_LIMIT = 52 * 1024 * 1024

F32 = jnp.float32
BF16 = jnp.bfloat16


def _cparams(*sem):
    return pltpu.CompilerParams(dimension_semantics=sem, vmem_limit_bytes=VMEM_LIMIT)


def _sigmoid(x):
    return 1.0 / (1.0 + jnp.exp(-x))


def _dot(a, b):
    return jnp.dot(a, b, preferred_element_type=F32)


def _dot_nt(a, b):
    return lax.dot_general(a, b, (((1,), (1,)), ((), ())), preferred_element_type=F32)


def _dot_tn(a, b):
    return lax.dot_general(a, b, (((0,), (0,)), ((), ())), preferred_element_type=F32)


def _split_bf16(x):
    hi = x.astype(BF16)
    lo = (x - hi.astype(F32)).astype(BF16)
    return hi, lo


def _rope_tables(pos, width):
    half = HEAD_DIM // 2
    inv = ROPE_THETA ** (-jnp.arange(half, dtype=F32) / half)
    ang = pos.astype(F32)[:, None] * inv[None, :]
    cos = jnp.concatenate([jnp.cos(ang), jnp.cos(ang)], axis=-1)
    sin = jnp.concatenate([-jnp.sin(ang), jnp.sin(ang)], axis=-1)
    reps = width // HEAD_DIM
    return jnp.tile(cos, (1, reps)), jnp.tile(sin, (1, reps))


def _rope128(x, cos, sin):
    lane = lax.broadcasted_iota(jnp.int32, x.shape, 1)
    first_half = (lane % HEAD_DIM) < (HEAD_DIM // 2)
    partner = jnp.where(first_half, pltpu.roll(x, LANES - HEAD_DIM // 2, 1),
                        pltpu.roll(x, HEAD_DIM // 2, 1))
    return x * cos + partner * sin


def _nsa_proj_kernel(x_ref, g_ref, wq_ref, wkv_ref, wg_ref, wz_ref, cos_ref, sin_ref, oh_ref, one_ref,
                     q_ref, q5_ref, kvc_ref, ka_ref, kw_ref, vs_ref, vwa_ref, gate_ref, sz_ref, *, tq):
    x = x_ref[...]
    h = x * lax.rsqrt(jnp.mean(x * x, axis=-1, keepdims=True) + EPS) * g_ref[...]
    hb = h.astype(BF16)
    cos = cos_ref[...]
    sin = sin_ref[...]
    hd, hpg = HEAD_DIM, NSA_HPG
    scale = HEAD_DIM ** -0.5 * LOG2E
    q = _dot(hb, wq_ref[...])
    for c in range(q.shape[1] // LANES):
        sl = slice(c * LANES, (c + 1) * LANES)
        slab = (_rope128(q[:, sl], cos, sin) * scale).astype(BF16)
        q_ref[:, sl] = slab
        for half in range(2):
            g, r = divmod(2 * c + half, hpg)
            for jq in range(slab.shape[0] // tq):
                q5_ref[0, g, jq, r * tq:(r + 1) * tq, :] = slab[jq * tq:(jq + 1) * tq, half * hd:(half + 1) * hd]
    kv = _dot(hb, wkv_ref[...])
    kvc_ref[...] = kv[:, 0:512]
    for c in range(2):
        ks = _rope128(kv[:, 512 + c * LANES:512 + (c + 1) * LANES], cos, sin).astype(BF16)
        kw = _rope128(kv[:, 1024 + c * LANES:1024 + (c + 1) * LANES], cos, sin).astype(BF16)
        for half in range(2):
            g = 2 * c + half
            lanes = slice(half * hd, (half + 1) * hd)
            ka_ref[0, g] = jnp.concatenate([ks[:, lanes], oh_ref[...]], axis=1)
            kw_ref[0, g] = kw[:, lanes]
    for g in range(NSA_GROUPS):
        vs_ref[0, g] = kv[:, 768 + g * hd:768 + (g + 1) * hd].astype(BF16)
        vwa_ref[0, g] = jnp.concatenate(
            [kv[:, 1280 + g * hd:1280 + (g + 1) * hd].astype(BF16), one_ref[...]], axis=1)
    gate_ref[...] = _sigmoid(_dot(hb, wg_ref[...]))
    z = _dot(hb, wz_ref[...])
    sz_ref[...] = (z * _sigmoid(z)).astype(BF16)


def _nsa_proj(x2, g, wq, wkv, wg, wz, cos, sin, onehot, ones_tile, S, tm, tq):
    n, d = x2.shape
    nt_s = S // tm
    B, G, R, hd = n // S, NSA_GROUPS, NSA_HPG, HEAD_DIM
    row = lambda i: (i, 0)
    fix = lambda i: (0, 0)
    pos = lambda i: (i % nt_s, 0)
    grp = lambda i: (i // nt_s, 0, i % nt_s, 0)
    return pl.pallas_call(
        functools.partial(_nsa_proj_kernel, tq=tq),
        grid=(n // tm,),
        in_specs=[
            pl.BlockSpec((tm, d), row),
            pl.BlockSpec((1, d), fix),
            pl.BlockSpec(wq.shape, fix),
            pl.BlockSpec(wkv.shape, fix),
            pl.BlockSpec(wg.shape, fix),
            pl.BlockSpec(wz.shape, fix),
            pl.BlockSpec((tm, LANES), pos),
            pl.BlockSpec((tm, LANES), pos),
            pl.BlockSpec((tm, SEL_GROUP), pos),
            pl.BlockSpec((tm, LANES - hd), fix),
        ],
        out_specs=[
            pl.BlockSpec((tm, 1024), row),
            pl.BlockSpec((1, G, tm // tq, R * tq, hd), lambda i: (i // nt_s, 0, i % nt_s, 0, 0)),
            pl.BlockSpec((tm, 512), row),
            pl.BlockSpec((1, G, tm, LANES), grp),
            pl.BlockSpec((1, G, tm, hd), grp),
            pl.BlockSpec((1, G, tm, hd), grp),
            pl.BlockSpec((1, G, tm, LANES), grp),
            pl.BlockSpec((tm, LANES), row),
            pl.BlockSpec((tm, 1024), row),
        ],
        out_shape=[
            jax.ShapeDtypeStruct((n, 1024), BF16),
            jax.ShapeDtypeStruct((B, G, S // tq, R * tq, hd), BF16),
            jax.ShapeDtypeStruct((n, 512), F32),
            jax.ShapeDtypeStruct((B, G, S, LANES), BF16),
            jax.ShapeDtypeStruct((B, G, S, hd), BF16),
            jax.ShapeDtypeStruct((B, G, S, hd), BF16),
            jax.ShapeDtypeStruct((B, G, S, LANES), BF16),
            jax.ShapeDtypeStruct((n, LANES), F32),
            jax.ShapeDtypeStruct((n, 1024), BF16),
        ],
        compiler_params=_cparams("parallel"),
        name="nsa_proj",
    )(x2, g, wq, wkv, wg, wz, cos, sin, onehot, ones_tile)


def _compress_kernel(a_ref, pe_ref, w1_ref, w2_ref, w2p_ref, cos_ref, sin_ref, o_ref):
    a = a_ref[0, 0]
    nc = a.shape[0]
    half = a.shape[1]
    lo = _dot((a + pe_ref[0:1, :]).astype(BF16), w1_ref[0:half, :])
    hi = _dot((a + pe_ref[1:2, :]).astype(BF16), w1_ref[half:2 * half, :])
    hid = lo + pltpu.roll(hi, nc - 1, 0)
    act = (hid * _sigmoid(hid)).astype(BF16)
    raw = _dot(act, w2_ref[...])
    partner = _dot(act, w2p_ref[...])
    o_ref[0, 0] = (raw * cos_ref[...] + partner * sin_ref[...]).astype(BF16)


def _compress(a, pe2, w1, w2, w2p, cos, sin):
    B, G, nc, f = a.shape
    fix = lambda b, g: (0, 0)
    return pl.pallas_call(
        _compress_kernel,
        grid=(B, G),
        in_specs=[
            pl.BlockSpec((1, 1, nc, f), lambda b, g: (b, g, 0, 0)),
            pl.BlockSpec(pe2.shape, fix),
            pl.BlockSpec(w1.shape, fix),
            pl.BlockSpec(w2.shape, fix),
            pl.BlockSpec(w2p.shape, fix),
            pl.BlockSpec(cos.shape, fix),
            pl.BlockSpec(sin.shape, fix),
        ],
        out_specs=pl.BlockSpec((1, 1, nc, HEAD_DIM), lambda b, g: (b, g, 0, 0)),
        out_shape=jax.ShapeDtypeStruct((B, G, nc, HEAD_DIM), BF16),
        compiler_params=_cparams("parallel", "parallel"),
        name="compress",
    )(a, pe2, w1, w2, w2p, cos, sin)


def _cmp_select_kernel(q_ref, k_ref, v_ref, ovt_ref, o_ref, bias_ref, *, tq, n_top):
    nc_all = k_ref.shape[2]
    n_bkt = CMP_BUCKETS if nc_all % (CMP_BUCKETS * LANES) == 0 else 1
    need = (pl.program_id(2) * tq + tq - CMP_BLOCK) // CMP_STRIDE + 1
    for bk in range(n_bkt):
        lo, nk = nc_all * bk // n_bkt, nc_all * (bk + 1) // n_bkt
        cond = need > lo if bk == n_bkt - 1 else (need <= nk if bk == 0 else (need > lo) & (need <= nk))

        @pl.when(cond)
        def _(nk=nk):
            for gg in range(q_ref.shape[1]):
                _cmp_select_group(q_ref.at[0, gg, 0], k_ref.at[0, gg, 0:nk], v_ref.at[0, gg, 0:nk],
                                  ovt_ref.at[:, 0:nk],
                                  o_ref.at[:, gg * NSA_HPG * HEAD_DIM:(gg + 1) * NSA_HPG * HEAD_DIM],
                                  bias_ref.at[0, gg, 0], tq=tq, n_top=n_top)


def _cmp_select_group(q_ref, k_ref, v_ref, ovt_ref, o_ref, bias_ref, *, tq, n_top):
    i = pl.program_id(2)
    q = q_ref[...]
    rows = q.shape[0]
    nc = k_ref.shape[0]
    t0 = i * tq
    s = _dot_nt(q, k_ref[...])
    t_col = t0 + lax.broadcasted_iota(jnp.int32, (rows, 1), 0) % tq
    last_ok = (t_col - (CMP_BLOCK - 1)) // CMP_STRIDE
    n_row = lax.broadcasted_iota(jnp.int32, (1, nc), 1)
    s = jnp.where(n_row <= last_ok, s, NEG)
    m = jnp.max(s, axis=-1, keepdims=True)
    e = jnp.exp2(s - m)
    ov = _dot(e.astype(BF16), v_ref[...])
    inv = jnp.where(t_col >= CMP_BLOCK - 1, 1.0 / ov[:, HEAD_DIM:HEAD_DIM + 1], 0.0)
    o = (ov[:, 0:HEAD_DIM] * inv).astype(o_ref.dtype)
    for r in range(rows // tq):
        o_ref[:, r * HEAD_DIM:(r + 1) * HEAD_DIM] = o[r * tq:(r + 1) * tq]

    p = e * inv
    ps = p[0:tq]
    for r in range(1, rows // tq):
        ps = ps + p[r * tq:(r + 1) * tq]
    ps_hi, ps_lo = _split_bf16(ps)
    ovt = ovt_ref[...]
    imp = _dot_nt(ovt, ps_hi) + _dot_nt(ovt, ps_lo)
    n_sel = imp.shape[0]
    j = lax.broadcasted_iota(jnp.int32, (n_sel, tq), 0)
    cur = (t0 + lax.broadcasted_iota(jnp.int32, (n_sel, tq), 1)) // SEL_BLOCK
    forced = (j == 0) | (j == cur) | (j == cur - 1)
    score = jnp.where(forced, -2.0, jnp.where(j <= cur, imp, -1.0))
    jf = j.astype(F32)
    for _ in range(max(n_top - 3, 0)):
        mx = jnp.max(score, axis=0, keepdims=True)
        pick = jnp.min(jnp.where(score == mx, jf, float(n_sel)), axis=0, keepdims=True)
        score = jnp.where(jf == pick, -2.0, score)
    keep = (score == -2.0) & (j < t0 // SEL_BLOCK)
    bias_ref[...] = jnp.where(keep, 0.0, NEG).astype(BF16)


def _cmp_select(q5, kc, vca, ovt, tq, n_top):
    B, G, nt, rows, hd = q5.shape
    nc = kc.shape[2]
    n_sel = ovt.shape[0]
    gp = GROUP_PAIR
    return pl.pallas_call(
        functools.partial(_cmp_select_kernel, tq=tq, n_top=n_top),
        grid=(B, G // gp, nt),
        in_specs=[
            pl.BlockSpec((1, gp, 1, rows, hd), lambda b, g, i: (b, g, i, 0, 0)),
            pl.BlockSpec((1, gp, nc, hd), lambda b, g, i: (b, g, 0, 0)),
            pl.BlockSpec((1, gp, nc, LANES), lambda b, g, i: (b, g, 0, 0)),
            pl.BlockSpec(ovt.shape, lambda b, g, i: (0, 0)),
        ],
        out_specs=[
            pl.BlockSpec((tq, gp * rows // tq * hd), lambda b, g, i: (b * nt + i, g)),
            pl.BlockSpec((1, gp, 1, n_sel, tq), lambda b, g, i: (b, g, i, 0, 0)),
        ],
        out_shape=[
            jax.ShapeDtypeStruct((B * nt * tq, G * rows // tq * hd), BF16),
            jax.ShapeDtypeStruct((B, G, nt, n_sel, tq), BF16),
        ],
        compiler_params=_cparams("parallel", "parallel", "parallel"),
        name="cmp_select",
    )(q5, kc, vca, ovt)


def _sel_attn_kernel(q_ref, bias_ref, k_ref, v_ref, o_ref, s_ref, acc_ref, m_ref, *, tq, tk):
    i = pl.program_id(2)
    gp = q_ref.shape[1]
    rows = q_ref.shape[4]
    reps = rows // tq
    t0 = pl.multiple_of(i * tq, tq)
    n_main = (t0 + tk - 1) // tk
    tiles_per_group = SEL_GROUP * SEL_BLOCK // tk

    def scores(gg, kt):
        g0 = pl.multiple_of((kt // tiles_per_group) * SEL_GROUP, SEL_GROUP)
        b = bias_ref[0, gg, 0, pl.ds(g0, SEL_GROUP), :]
        qa = jnp.concatenate([q_ref[0, gg, 0], jnp.concatenate([b] * reps, axis=1)], axis=0)
        k = k_ref[0, gg, pl.ds(pl.multiple_of(kt * tk, tk), tk), :]
        return _dot(k, qa)

    def update(gg, slot, kt):
        s = s_ref[gg, slot]
        m = m_ref[gg]
        m_new = jnp.maximum(m, jnp.max(s, axis=0, keepdims=True))
        p = jnp.exp2(s - m_new).astype(BF16)
        v = v_ref[0, gg, :, pl.ds(pl.multiple_of(kt * tk, tk), tk)]
        acc_ref[gg] = jnp.exp2(m - m_new) * acc_ref[gg] + _dot(v, p)
        m_ref[gg] = m_new

    kpos = t0 + lax.broadcasted_iota(jnp.int32, (tq, rows), 0)
    t_col = t0 + lax.broadcasted_iota(jnp.int32, (tq, rows), 1) % tq
    for gg in range(gp):
        s = _dot(k_ref[0, gg, pl.ds(t0, tq), 0:HEAD_DIM], q_ref[0, gg, 0])
        s = jnp.where(kpos <= t_col, s, NEG)
        m = jnp.max(s, axis=0, keepdims=True)
        m_ref[gg] = m
        acc_ref[gg] = _dot(v_ref[0, gg, :, pl.ds(t0, tq)], jnp.exp2(s - m).astype(BF16))
        s_ref[gg, 0] = scores(gg, 0)

    def body(jj, carry):
        k0 = 2 * jj
        for gg in range(gp):
            s_ref[gg, 1] = scores(gg, k0 + 1)
        for gg in range(gp):
            update(gg, 0, k0)
        for gg in range(gp):
            s_ref[gg, 0] = scores(gg, jnp.minimum(k0 + 2, n_main - 1))
        for gg in range(gp):
            update(gg, 1, k0 + 1)
        return carry

    lax.fori_loop(0, n_main // 2, body, 0)

    @pl.when(n_main % 2 == 1)
    def _():
        for gg in range(gp):
            update(gg, 0, n_main - 1)

    for gg in range(gp):
        o = acc_ref[gg, 0:HEAD_DIM, :] / acc_ref[gg, HEAD_DIM:HEAD_DIM + 1, :]
        for r in range(reps):
            c0 = (gg * reps + r) * HEAD_DIM
            o_ref[:, c0:c0 + HEAD_DIM] = o[:, r * tq:(r + 1) * tq].T.astype(o_ref.dtype)


def _sel_attn(qt, bias_t, ka, vat, tq, tk):
    B, G, nt, hd, rows = qt.shape
    S = ka.shape[2]
    nb = bias_t.shape[3]
    gp = GROUP_PAIR
    return pl.pallas_call(
        functools.partial(_sel_attn_kernel, tq=tq, tk=tk),
        grid=(B, G // gp, nt),
        in_specs=[
            pl.BlockSpec((1, gp, 1, hd, rows), lambda b, g, i: (b, g, i, 0, 0)),
            pl.BlockSpec((1, gp, 1, nb, tq), lambda b, g, i: (b, g, i, 0, 0)),
            pl.BlockSpec((1, gp, S, LANES), lambda b, g, i: (b, g, 0, 0)),
            pl.BlockSpec((1, gp, PV_ROWS, S), lambda b, g, i: (b, g, 0, 0)),
        ],
        out_specs=pl.BlockSpec((tq, gp * rows // tq * hd), lambda b, g, i: (b * nt + i, g)),
        out_shape=jax.ShapeDtypeStruct((B * nt * tq, G * rows // tq * hd), BF16),
        scratch_shapes=[pltpu.VMEM((gp, 2, tk, rows), F32),
                        pltpu.VMEM((gp, PV_ROWS, rows), F32),
                        pltpu.VMEM((gp, 1, rows), F32)],
        compiler_params=_cparams("parallel", "parallel", "arbitrary"),
        name="sel_attn",
    )(qt, bias_t, ka, vat)


def _win_attn_kernel(q_ref, k_ref, v_ref, o_ref, *, tq):
    for gg in range(q_ref.shape[1]):
        _win_attn_group(q_ref.at[0, gg, 0], k_ref.at[0, gg], v_ref.at[0, gg],
                        o_ref.at[:, gg * NSA_HPG * HEAD_DIM:(gg + 1) * NSA_HPG * HEAD_DIM], tq=tq)


def _win_attn_group(q_ref, k_ref, v_ref, o_ref, *, tq):
    i = pl.program_id(2)
    q = q_ref[...]
    rows = q.shape[0]
    span = WINDOW + tq
    t0 = i * tq
    start = pl.multiple_of(jnp.maximum(t0 - WINDOW, 0), tq)
    k = k_ref[pl.ds(start, span), :]
    v = v_ref[pl.ds(start, span), :]
    s = _dot_nt(q, k)
    t_col = t0 + lax.broadcasted_iota(jnp.int32, (rows, 1), 0) % tq
    kpos = start + lax.broadcasted_iota(jnp.int32, (1, span), 1)
    dlt = lax.bitcast_convert_type(t_col - kpos, jnp.uint32)
    s = jnp.where(dlt < jnp.uint32(WINDOW), s, NEG)
    m = jnp.max(s, axis=-1, keepdims=True)
    ov = _dot(jnp.exp2(s - m).astype(BF16), v)
    o = (ov[:, 0:HEAD_DIM] / ov[:, HEAD_DIM:HEAD_DIM + 1]).astype(o_ref.dtype)
    for r in range(rows // tq):
        o_ref[:, r * HEAD_DIM:(r + 1) * HEAD_DIM] = o[r * tq:(r + 1) * tq]


def _win_attn(q5, kw, vwa, tq):
    B, G, nt, rows, hd = q5.shape
    S = kw.shape[2]
    gp = GROUP_PAIR
    return pl.pallas_call(
        functools.partial(_win_attn_kernel, tq=tq),
        grid=(B, G // gp, nt),
        in_specs=[
            pl.BlockSpec((1, gp, 1, rows, hd), lambda b, g, i: (b, g, i, 0, 0)),
            pl.BlockSpec((1, gp, S, hd), lambda b, g, i: (b, g, 0, 0)),
            pl.BlockSpec((1, gp, S, LANES), lambda b, g, i: (b, g, 0, 0)),
        ],
        out_specs=pl.BlockSpec((tq, gp * rows // tq * hd), lambda b, g, i: (b * nt + i, g)),
        out_shape=jax.ShapeDtypeStruct((B * nt * tq, G * rows // tq * hd), BF16),
        compiler_params=_cparams("parallel", "parallel", "arbitrary"),
        name="win_attn",
    )(q5, kw, vwa)


def _nsa_out_kernel(x_ref, oc_ref, os_ref, ow_ref, gate_ref, ex_ref, sz_ref, w_ref, y_ref):
    g_hi, g_lo = _split_bf16(gate_ref[...])
    o = None
    for c, o_ref in enumerate((oc_ref, os_ref, ow_ref)):
        e = ex_ref[c]
        term = (_dot(g_hi, e) + _dot(g_lo, e)) * o_ref[...].astype(F32)
        o = term if o is None else o + term
    u = (o * sz_ref[...].astype(F32)).astype(BF16)
    y_ref[...] = x_ref[...] + _dot(u, w_ref[...])


def _nsa_out(x2, oc, os_, ow, gates, expand, sz, w_out, tm):
    n, d = x2.shape
    row = lambda i: (i, 0)
    return pl.pallas_call(
        _nsa_out_kernel,
        grid=(n // tm,),
        in_specs=[
            pl.BlockSpec((tm, d), row),
            pl.BlockSpec((tm, d), row),
            pl.BlockSpec((tm, d), row),
            pl.BlockSpec((tm, d), row),
            pl.BlockSpec((tm, LANES), row),
            pl.BlockSpec(expand.shape, lambda i: (0, 0, 0)),
            pl.BlockSpec((tm, d), row),
            pl.BlockSpec(w_out.shape, lambda i: (0, 0)),
        ],
        out_specs=pl.BlockSpec((tm, d), row),
        out_shape=jax.ShapeDtypeStruct((n, d), F32),
        compiler_params=_cparams("parallel"),
        name="nsa_out",
    )(x2, oc, os_, ow, gates, expand, sz, w_out)


def _hg_proj_kernel(x_ref, g_ref, lb_ref, wq_ref, wf_ref, wi_ref, wz_ref,
                    q_ref, k_ref, lf_ref, v_ref, sz_ref):
    x = x_ref[...]
    h = x * lax.rsqrt(jnp.mean(x * x, axis=-1, keepdims=True) + EPS) * g_ref[...]
    hb = h.astype(BF16)
    qr = _dot(hb, wq_ref[...])
    q_ref[...] = (qr * _sigmoid(qr)).astype(BF16)
    lb = lb_ref[...]
    f = lb + (1.0 - lb) * _sigmoid(_dot(hb, wf_ref[...]))
    k_ref[...] = (1.0 - f).astype(BF16)
    lf_ref[...] = jnp.log(f)
    v_ref[...] = _dot(hb, wi_ref[...]).astype(BF16)
    z = _dot(hb, wz_ref[...])
    sz_ref[...] = (z * _sigmoid(z)).astype(BF16)


def _hg_proj(x2, g, lb, wq, wf, wi, wz, tm):
    n, d = x2.shape
    row = lambda i: (i, 0)
    fix = lambda i: (0, 0)
    return pl.pallas_call(
        _hg_proj_kernel,
        grid=(n // tm,),
        in_specs=[pl.BlockSpec((tm, d), row), pl.BlockSpec((1, d), fix), pl.BlockSpec((1, d), fix)]
        + [pl.BlockSpec((d, d), fix)] * 4,
        out_specs=[pl.BlockSpec((tm, d), row)] * 5,
        out_shape=[
            jax.ShapeDtypeStruct((n, d), BF16),
            jax.ShapeDtypeStruct((n, d), BF16),
            jax.ShapeDtypeStruct((n, d), F32),
            jax.ShapeDtypeStruct((n, d), BF16),
            jax.ShapeDtypeStruct((n, d), BF16),
        ],
        compiler_params=_cparams("parallel"),
        name="hg_proj",
    )(x2, g, lb, wq, wf, wi, wz)


def _hg_masks(c):
    t = np.arange(c)
    masks = []
    h = 1
    while h < c:
        same = (t[:, None] // (2 * h)) == (t[None, :] // (2 * h))
        upper = ((t[:, None] // h) % 2) == 1
        lower = ((t[None, :] // h) % 2) == 0
        masks.append((same & upper & lower).astype(np.float32))
        h *= 2
    masks.append(np.eye(c, dtype=np.float32))
    return np.stack(masks, axis=0)


def _hg_scan_kernel(q_ref, k_ref, lf_ref, v_ref, gn_ref, mk_ref, o_ref, st_ref, qd_ref, kd_ref, *, c):
    @pl.when(pl.program_id(2) == 0)
    def _():
        st_ref[...] = jnp.zeros_like(st_ref)

    ct = q_ref.shape[1]
    n_lvl = mk_ref.shape[0] - 1
    lf = lf_ref[0]
    pos = lax.broadcasted_iota(jnp.int32, lf.shape, 0) % c
    b = lf
    d = 1
    while d < c:
        b = b + jnp.where(pos >= d, pltpu.roll(b, d, 0), 0.0)
        d *= 2
    q = q_ref[0].astype(F32)
    k = k_ref[0].astype(F32)
    start, end = b - lf, b
    h = 1
    for lv in range(n_lvl + 1):
        qd_ref[lv] = (q * jnp.exp(b - start)).astype(BF16)
        kd_ref[lv] = (k * jnp.exp(end - b)).astype(BF16)
        if h < c:
            odd = (pos // h) % 2 == 1
            start = jnp.where(odd, pltpu.roll(start, h, 0), start)
            end = jnp.where(odd, end, pltpu.roll(end, ct - h, 0))
            h *= 2
    chunk_decay = jnp.exp(end)

    st = st_ref[...]
    for ci in range(ct // c):
        sl = slice(ci * c, (ci + 1) * c)
        v = v_ref[0, sl, :]
        att = mk_ref[n_lvl] * _dot_nt(q_ref[0, sl, :], k_ref[0, sl, :])
        for lv in range(n_lvl):
            att = att + mk_ref[lv] * _dot_nt(qd_ref[lv, sl, :], kd_ref[lv, sl, :])
        o = _dot_nt(qd_ref[n_lvl, sl, :], st.astype(BF16)) + _dot(att.astype(BF16), v)
        st = st * chunk_decay[ci * c:ci * c + 1, :] + _dot_tn(v, kd_ref[n_lvl, sl, :])
        on = o * lax.rsqrt(jnp.mean(o * o, axis=-1, keepdims=True) + EPS)
        o_ref[0, sl, :] = (on * gn_ref[...]).astype(o_ref.dtype)
    st_ref[...] = st


def _hg_scan(q, k, lf, v, gn, masks, ct):
    B, S, d = q.shape
    nh = d // HG_HEAD_DIM
    n_parts = masks.shape[0]
    blk = pl.BlockSpec((1, ct, HG_HEAD_DIM), lambda b, h, i: (b, i, h))
    return pl.pallas_call(
        functools.partial(_hg_scan_kernel, c=HG_CHUNK),
        grid=(B, nh, S // ct),
        in_specs=[blk, blk, blk, blk,
                  pl.BlockSpec((1, HG_HEAD_DIM), lambda b, h, i: (0, h)),
                  pl.BlockSpec(masks.shape, lambda b, h, i: (0, 0, 0))],
        out_specs=blk,
        out_shape=jax.ShapeDtypeStruct((B, S, d), BF16),
        scratch_shapes=[pltpu.VMEM((HG_HEAD_DIM, HG_HEAD_DIM), F32),
                        pltpu.VMEM((n_parts, ct, HG_HEAD_DIM), BF16),
                        pltpu.VMEM((n_parts, ct, HG_HEAD_DIM), BF16)],
        compiler_params=_cparams("parallel", "parallel", "arbitrary"),
        name="hg_scan",
    )(q, k, lf, v, gn, masks)


def _hg_out_kernel(x_ref, o_ref, sz_ref, w_ref, g_ref, y_ref):
    u = (o_ref[...].astype(F32) * sz_ref[...].astype(F32)).astype(BF16)
    x = x_ref[...] + _dot(u, w_ref[...])
    y_ref[...] = x * lax.rsqrt(jnp.mean(x * x, axis=-1, keepdims=True) + EPS) * g_ref[...]


def _hg_out(x2, o, sz, w_out, g, tm):
    n, d = x2.shape
    row = lambda i: (i, 0)
    fix = lambda i: (0, 0)
    return pl.pallas_call(
        _hg_out_kernel,
        grid=(n // tm,),
        in_specs=[pl.BlockSpec((tm, d), row), pl.BlockSpec((tm, d), row), pl.BlockSpec((tm, d), row),
                  pl.BlockSpec((d, d), fix), pl.BlockSpec((1, d), fix)],
        out_specs=pl.BlockSpec((tm, d), row),
        out_shape=jax.ShapeDtypeStruct((n, d), F32),
        compiler_params=_cparams("parallel"),
        name="hg_out",
    )(x2, o, sz, w_out, g)


def _nsa_layer(x2, B, S, g, w_in, pe_k, pe_v, wk1, wk2, wv1, wv2, w_out):
    H, G, R, hd = NSA_HEADS, NSA_GROUPS, NSA_HPG, HEAD_DIM
    n = B * S
    tm = 512
    tq = 128
    qw, kvw = H * hd, G * hd
    off = np.cumsum([0, qw] + [kvw] * 6 + [3 * H, qw])
    wq = w_in[:, off[0]:off[1]].astype(BF16)
    wkv = w_in[:, off[1]:off[7]].astype(BF16)
    wg = jnp.pad(w_in[:, off[7]:off[8]], ((0, 0), (0, LANES - 3 * H))).astype(BF16)
    wz = w_in[:, off[8]:off[9]].astype(BF16)
    pos = jnp.arange(S)
    cos, sin = _rope_tables(pos, LANES)
    onehot = jax.nn.one_hot((pos // SEL_BLOCK) % SEL_GROUP, SEL_GROUP, dtype=BF16)
    ones_col = jnp.zeros((1, LANES - hd), BF16).at[0, 0].set(1.0)
    q, q5, kvc, ka, kw, vs, vwa, gates, sz = _nsa_proj(
        x2, g[None, :], wq, wkv, wg, wz, cos, sin, onehot, jnp.broadcast_to(ones_col, (tm, LANES - hd)),
        S, tm, tq)

    nc = S // CMP_STRIDE
    per = CMP_BLOCK // CMP_STRIDE
    assert per == 2
    def chunks(a):
        a = a.reshape(B, nc, CMP_STRIDE, G, hd).transpose(0, 3, 1, 2, 4)
        return a.reshape(B, G, nc, CMP_STRIDE * hd)
    cmp_end = jnp.arange(nc) * CMP_STRIDE + CMP_BLOCK - 1
    ccos, csin = _rope_tables(cmp_end, hd)
    swap = lambda w: jnp.concatenate([w[:, hd // 2:], w[:, :hd // 2]], axis=1)
    k_cmp = _compress(chunks(kvc[:, 0:kvw]), pe_k.reshape(per, CMP_STRIDE * hd), wk1.astype(BF16),
                      wk2.astype(BF16), swap(wk2).astype(BF16), ccos, csin)
    v_cmp = _compress(chunks(kvc[:, kvw:2 * kvw]), pe_v.reshape(per, CMP_STRIDE * hd), wv1.astype(BF16),
                      wv2.astype(BF16), swap(wv2).astype(BF16),
                      jnp.ones_like(ccos), jnp.zeros_like(csin))

    nt = S // tq

    n_sel = S // SEL_BLOCK
    n_top = min(SEL_TOPK, n_sel)
    cs = np.arange(nc)[:, None] * CMP_STRIDE
    ss = np.arange(n_sel)[None, :] * SEL_BLOCK
    overlap = ((cs < ss + SEL_BLOCK) & (cs + CMP_BLOCK > ss)).astype(np.float32)
    ovt = jnp.asarray(overlap.T, dtype=BF16)
    with_ones = lambda v: jnp.concatenate(
        [v, jnp.broadcast_to(ones_col, v.shape[:-1] + (LANES - hd,))], axis=-1)
    o_c, bias_t = _cmp_select(q5, k_cmp, with_ones(v_cmp), ovt, tq, n_top)

    nsg = -(-n_sel // SEL_GROUP)
    bias_t = jnp.pad(bias_t, ((0, 0),) * 3 + ((0, nsg * SEL_GROUP - n_sel), (0, 0)), constant_values=NEG)
    qt = q.reshape(B, nt, tq, G, R, hd).transpose(0, 3, 1, 5, 4, 2).reshape(B, G, nt, hd, R * tq)
    ones_row = jnp.zeros((PV_ROWS - hd, S), BF16).at[0, :].set(1.0)
    vat = jnp.concatenate([jnp.swapaxes(vs, 2, 3), jnp.broadcast_to(ones_row, (B, G, PV_ROWS - hd, S))], axis=2)
    tk = min(512, S)
    o_s = _sel_attn(qt, bias_t, ka, vat, tq, tk)
    o_w = _win_attn(q5, kw, vwa, tq)

    col = np.arange(3 * H)
    expand = np.zeros((3, LANES, qw), np.float32)
    for c in range(3):
        head = col[col % 3 == c] // 3
        for hh, cc in zip(head, col[col % 3 == c]):
            expand[c, cc, hh * hd:(hh + 1) * hd] = 1.0
    return _nsa_out(x2, o_c, o_s, o_w, gates,
                    jnp.asarray(expand, dtype=BF16), sz, w_out.astype(BF16), tm)


def _hgrn_layer(x2, B, S, g, w_in, lb, g_norm, w_out, final_g):
    n, d = x2.shape
    tm = 512
    wq, wf, wi, wz = (w_in[:, i * d:(i + 1) * d].astype(BF16) for i in range(4))
    q, k, lf, v, sz = _hg_proj(x2, g[None, :], lb[None, :], wq, wf, wi, wz, tm)
    r3 = lambda a: a.reshape(B, S, d)
    o = _hg_scan(r3(q), r3(k), r3(lf), r3(v), g_norm[None, :], jnp.asarray(_hg_masks(HG_CHUNK)),
                 min(512, S))
    return _hg_out(x2, o.reshape(n, d), sz, w_out.astype(BF16), final_g[None, :], tm)


def kernel(x, norm_w, nsa_w_in, nsa_pe_k, nsa_pe_v, nsa_wk1, nsa_wk2, nsa_wv1, nsa_wv2,
           nsa_w_out, hg_w_in, hg_lb_logits, hg_norm, hg_w_out, final_norm):
    B, S, d = x.shape
    depth = norm_w.shape[0]
    assert depth == 2 and nsa_w_in.shape[0] == 1 and hg_w_in.shape[0] == 1
    p = jax.nn.softmax(hg_lb_logits.astype(F32), axis=0)
    lower_bounds = jnp.cumsum(p, axis=0) - p[0]
    x2 = x.reshape(B * S, d)
    x2 = _nsa_layer(x2, B, S, norm_w[0], nsa_w_in[0], nsa_pe_k[0], nsa_pe_v[0], nsa_wk1[0],
                    nsa_wk2[0], nsa_wv1[0], nsa_wv2[0], nsa_w_out[0])
    out = _hgrn_layer(x2, B, S, norm_w[1], hg_w_in[0], lower_bounds[1], hg_norm[0], hg_w_out[0],
                      final_norm)
    return out.reshape(B, S, d)
```

```python
import functools

import jax
import jax.numpy as jnp
import numpy as np
from jax import lax
from jax.experimental import pallas as pl
from jax.experimental.pallas import tpu as pltpu

EPS = 1e-6
ROPE_THETA = 10000.0

NSA_HEADS = 16
NSA_GROUPS = 4
NSA_HPG = NSA_HEADS // NSA_GROUPS
HEAD_DIM = 64
CMP_BLOCK = 32
CMP_STRIDE = 16
SEL_BLOCK = 64
SEL_TOPK = 16
WINDOW = 512

HG_HEAD_DIM = 128
HG_CHUNK = 64

NEG = -1e30
LOG2E = 1.4426950408889634
LANES = 128
SEL_GROUP = 64
GROUP_PAIR = 2
CMP_GROUPS = 4
CMP_BUCKETS = 4
PV_ROWS = HEAD_DIM + 16
VMEM_LIMIT = 52 * 1024 * 1024

F32 = jnp.float32
BF16 = jnp.bfloat16


def _cparams(*sem):
    return pltpu.CompilerParams(dimension_semantics=sem, vmem_limit_bytes=VMEM_LIMIT)


def _sigmoid(x):
    return 1.0 / (1.0 + jnp.exp(-x))


def _dot(a, b):
    return jnp.dot(a, b, preferred_element_type=F32)


def _dot_nt(a, b):
    return lax.dot_general(a, b, (((1,), (1,)), ((), ())), preferred_element_type=F32)


def _dot_tn(a, b):
    return lax.dot_general(a, b, (((0,), (0,)), ((), ())), preferred_element_type=F32)


def _split_bf16(x):
    hi = x.astype(BF16)
    lo = (x - hi.astype(F32)).astype(BF16)
    return hi, lo


def _rope_tables(pos, width):
    half = HEAD_DIM // 2
    inv = ROPE_THETA ** (-jnp.arange(half, dtype=F32) / half)
    ang = pos.astype(F32)[:, None] * inv[None, :]
    cos = jnp.concatenate([jnp.cos(ang), jnp.cos(ang)], axis=-1)
    sin = jnp.concatenate([-jnp.sin(ang), jnp.sin(ang)], axis=-1)
    reps = width // HEAD_DIM
    return jnp.tile(cos, (1, reps)), jnp.tile(sin, (1, reps))


def _rope128(x, cos, sin):
    lane = lax.broadcasted_iota(jnp.int32, x.shape, 1)
    first_half = (lane % HEAD_DIM) < (HEAD_DIM // 2)
    partner = jnp.where(first_half, pltpu.roll(x, LANES - HEAD_DIM // 2, 1),
                        pltpu.roll(x, HEAD_DIM // 2, 1))
    return x * cos + partner * sin


def _nsa_proj_kernel(x_ref, g_ref, wq_ref, wkv_ref, wg_ref, wz_ref, cos_ref, sin_ref, oh_ref, one_ref,
                     q_ref, q5_ref, kvc_ref, ka_ref, kw_ref, vs_ref, vwa_ref, gate_ref, sz_ref, *, tq):
    x = x_ref[...]
    h = x * lax.rsqrt(jnp.mean(x * x, axis=-1, keepdims=True) + EPS) * g_ref[...]
    hb = h.astype(BF16)
    cos = cos_ref[...]
    sin = sin_ref[...]
    hd, hpg = HEAD_DIM, NSA_HPG
    scale = HEAD_DIM ** -0.5 * LOG2E
    q = _dot(hb, wq_ref[...])
    for c in range(q.shape[1] // LANES):
        sl = slice(c * LANES, (c + 1) * LANES)
        slab = (_rope128(q[:, sl], cos, sin) * scale).astype(BF16)
        q_ref[:, sl] = slab
        for half in range(2):
            g, r = divmod(2 * c + half, hpg)
            for jq in range(slab.shape[0] // tq):
                q5_ref[0, g, jq, r * tq:(r + 1) * tq, :] = slab[jq * tq:(jq + 1) * tq, half * hd:(half + 1) * hd]
    kv = _dot(hb, wkv_ref[...])
    kvc_ref[...] = kv[:, 0:512]
    for c in range(2):
        ks = _rope128(kv[:, 512 + c * LANES:512 + (c + 1) * LANES], cos, sin).astype(BF16)
        kw = _rope128(kv[:, 1024 + c * LANES:1024 + (c + 1) * LANES], cos, sin).astype(BF16)
        for half in range(2):
            g = 2 * c + half
            lanes = slice(half * hd, (half + 1) * hd)
            ka_ref[0, g] = jnp.concatenate([ks[:, lanes], oh_ref[...]], axis=1)
            kw_ref[0, g] = kw[:, lanes]
    for g in range(NSA_GROUPS):
        vs_ref[0, g] = kv[:, 768 + g * hd:768 + (g + 1) * hd].astype(BF16)
        vwa_ref[0, g] = jnp.concatenate(
            [kv[:, 1280 + g * hd:1280 + (g + 1) * hd].astype(BF16), one_ref[...]], axis=1)
    gate_ref[...] = _sigmoid(_dot(hb, wg_ref[...]))
    z = _dot(hb, wz_ref[...])
    sz_ref[...] = (z * _sigmoid(z)).astype(BF16)


def _nsa_proj(x2, g, wq, wkv, wg, wz, cos, sin, onehot, ones_tile, S, tm, tq):
    n, d = x2.shape
    nt_s = S // tm
    B, G, R, hd = n // S, NSA_GROUPS, NSA_HPG, HEAD_DIM
    row = lambda i: (i, 0)
    fix = lambda i: (0, 0)
    pos = lambda i: (i % nt_s, 0)
    grp = lambda i: (i // nt_s, 0, i % nt_s, 0)
    return pl.pallas_call(
        functools.partial(_nsa_proj_kernel, tq=tq),
        grid=(n // tm,),
        in_specs=[
            pl.BlockSpec((tm, d), row),
            pl.BlockSpec((1, d), fix),
            pl.BlockSpec(wq.shape, fix),
            pl.BlockSpec(wkv.shape, fix),
            pl.BlockSpec(wg.shape, fix),
            pl.BlockSpec(wz.shape, fix),
            pl.BlockSpec((tm, LANES), pos),
            pl.BlockSpec((tm, LANES), pos),
            pl.BlockSpec((tm, SEL_GROUP), pos),
            pl.BlockSpec((tm, LANES - hd), fix),
        ],
        out_specs=[
            pl.BlockSpec((tm, 1024), row),
            pl.BlockSpec((1, G, tm // tq, R * tq, hd), lambda i: (i // nt_s, 0, i % nt_s, 0, 0)),
            pl.BlockSpec((tm, 512), row),
            pl.BlockSpec((1, G, tm, LANES), grp),
            pl.BlockSpec((1, G, tm, hd), grp),
            pl.BlockSpec((1, G, tm, hd), grp),
            pl.BlockSpec((1, G, tm, LANES), grp),
            pl.BlockSpec((tm, LANES), row),
            pl.BlockSpec((tm, 1024), row),
        ],
        out_shape=[
            jax.ShapeDtypeStruct((n, 1024), BF16),
            jax.ShapeDtypeStruct((B, G, S // tq, R * tq, hd), BF16),
            jax.ShapeDtypeStruct((n, 512), F32),
            jax.ShapeDtypeStruct((B, G, S, LANES), BF16),
            jax.ShapeDtypeStruct((B, G, S, hd), BF16),
            jax.ShapeDtypeStruct((B, G, S, hd), BF16),
            jax.ShapeDtypeStruct((B, G, S, LANES), BF16),
            jax.ShapeDtypeStruct((n, LANES), F32),
            jax.ShapeDtypeStruct((n, 1024), BF16),
        ],
        compiler_params=_cparams("parallel"),
        name="nsa_proj",
    )(x2, g, wq, wkv, wg, wz, cos, sin, onehot, ones_tile)


def _compress_kernel(a_ref, pe_ref, w1_ref, w2_ref, w2p_ref, cos_ref, sin_ref, o_ref):
    a = a_ref[0, 0]
    nc = a.shape[0]
    half = a.shape[1]
    lo = _dot((a + pe_ref[0:1, :]).astype(BF16), w1_ref[0:half, :])
    hi = _dot((a + pe_ref[1:2, :]).astype(BF16), w1_ref[half:2 * half, :])
    hid = lo + pltpu.roll(hi, nc - 1, 0)
    act = (hid * _sigmoid(hid)).astype(BF16)
    raw = _dot(act, w2_ref[...])
    partner = _dot(act, w2p_ref[...])
    o_ref[0, 0] = (raw * cos_ref[...] + partner * sin_ref[...]).astype(BF16)


def _compress(a, pe2, w1, w2, w2p, cos, sin):
    B, G, nc, f = a.shape
    fix = lambda b, g: (0, 0)
    return pl.pallas_call(
        _compress_kernel,
        grid=(B, G),
        in_specs=[
            pl.BlockSpec((1, 1, nc, f), lambda b, g: (b, g, 0, 0)),
            pl.BlockSpec(pe2.shape, fix),
            pl.BlockSpec(w1.shape, fix),
            pl.BlockSpec(w2.shape, fix),
            pl.BlockSpec(w2p.shape, fix),
            pl.BlockSpec(cos.shape, fix),
            pl.BlockSpec(sin.shape, fix),
        ],
        out_specs=pl.BlockSpec((1, 1, nc, HEAD_DIM), lambda b, g: (b, g, 0, 0)),
        out_shape=jax.ShapeDtypeStruct((B, G, nc, HEAD_DIM), BF16),
        compiler_params=_cparams("parallel", "parallel"),
        name="compress",
    )(a, pe2, w1, w2, w2p, cos, sin)


def _cmp_select_kernel(q_ref, k_ref, v_ref, ovt_ref, o_ref, bias_ref, *, tq, n_top):
    nc_all = k_ref.shape[2]
    n_bkt = CMP_BUCKETS if nc_all % (CMP_BUCKETS * LANES) == 0 else 1
    need = (pl.program_id(2) * tq + tq - CMP_BLOCK) // CMP_STRIDE + 1
    for bk in range(n_bkt):
        lo, nk = nc_all * bk // n_bkt, nc_all * (bk + 1) // n_bkt
        cond = need > lo if bk == n_bkt - 1 else (need <= nk if bk == 0 else (need > lo) & (need <= nk))

        @pl.when(cond)
        def _(nk=nk):
            for gg in range(q_ref.shape[1]):
                _cmp_select_group(q_ref.at[0, gg, 0], k_ref.at[0, gg, 0:nk], v_ref.at[0, gg, 0:nk],
                                  ovt_ref.at[:, 0:nk],
                                  o_ref.at[:, gg * NSA_HPG * HEAD_DIM:(gg + 1) * NSA_HPG * HEAD_DIM],
                                  bias_ref.at[0, gg, 0], tq=tq, n_top=n_top)


def _cmp_select_group(q_ref, k_ref, v_ref, ovt_ref, o_ref, bias_ref, *, tq, n_top):
    i = pl.program_id(2)
    q = q_ref[...]
    rows = q.shape[0]
    nc = k_ref.shape[0]
    t0 = i * tq
    s = _dot_nt(q, k_ref[...])
    t_col = t0 + lax.broadcasted_iota(jnp.int32, (rows, 1), 0) % tq
    last_ok = (t_col - (CMP_BLOCK - 1)) // CMP_STRIDE
    n_row = lax.broadcasted_iota(jnp.int32, (1, nc), 1)
    s = jnp.where(n_row <= last_ok, s, NEG)
    m = jnp.max(s, axis=-1, keepdims=True)
    e = jnp.exp2(s - m)
    ov = _dot(e.astype(BF16), v_ref[...])
    inv = jnp.where(t_col >= CMP_BLOCK - 1, 1.0 / ov[:, HEAD_DIM:HEAD_DIM + 1], 0.0)
    o = (ov[:, 0:HEAD_DIM] * inv).astype(o_ref.dtype)
    for r in range(rows // tq):
        o_ref[:, r * HEAD_DIM:(r + 1) * HEAD_DIM] = o[r * tq:(r + 1) * tq]

    p = e * inv
    ps = p[0:tq]
    for r in range(1, rows // tq):
        ps = ps + p[r * tq:(r + 1) * tq]
    ps_hi, ps_lo = _split_bf16(ps)
    ovt = ovt_ref[...]
    imp = _dot_nt(ovt, ps_hi) + _dot_nt(ovt, ps_lo)
    n_sel = imp.shape[0]
    j = lax.broadcasted_iota(jnp.int32, (n_sel, tq), 0)
    cur = (t0 + lax.broadcasted_iota(jnp.int32, (n_sel, tq), 1)) // SEL_BLOCK
    forced = (j == 0) | (j == cur) | (j == cur - 1)
    score = jnp.where(forced, -2.0, jnp.where(j <= cur, imp, -1.0))
    jf = j.astype(F32)
    for _ in range(max(n_top - 3, 0)):
        mx = jnp.max(score, axis=0, keepdims=True)
        pick = jnp.min(jnp.where(score == mx, jf, float(n_sel)), axis=0, keepdims=True)
        score = jnp.where(jf == pick, -2.0, score)
    keep = (score == -2.0) & (j < t0 // SEL_BLOCK)
    bias_ref[...] = jnp.where(keep, 0.0, NEG).astype(BF16)


def _cmp_select(q5, kc, vca, ovt, tq, n_top):
    B, G, nt, rows, hd = q5.shape
    nc = kc.shape[2]
    n_sel = ovt.shape[0]
    gp = CMP_GROUPS
    return pl.pallas_call(
        functools.partial(_cmp_select_kernel, tq=tq, n_top=n_top),
        grid=(B, G // gp, nt),
        in_specs=[
            pl.BlockSpec((1, gp, 1, rows, hd), lambda b, g, i: (b, g, i, 0, 0)),
            pl.BlockSpec((1, gp, nc, hd), lambda b, g, i: (b, g, 0, 0)),
            pl.BlockSpec((1, gp, nc, LANES), lambda b, g, i: (b, g, 0, 0)),
            pl.BlockSpec(ovt.shape, lambda b, g, i: (0, 0)),
        ],
        out_specs=[
            pl.BlockSpec((tq, gp * rows // tq * hd), lambda b, g, i: (b * nt + i, g)),
            pl.BlockSpec((1, gp, 1, n_sel, tq), lambda b, g, i: (b, g, i, 0, 0)),
        ],
        out_shape=[
            jax.ShapeDtypeStruct((B * nt * tq, G * rows // tq * hd), BF16),
            jax.ShapeDtypeStruct((B, G, nt, n_sel, tq), BF16),
        ],
        compiler_params=_cparams("parallel", "parallel", "parallel"),
        name="cmp_select",
    )(q5, kc, vca, ovt)


def _sel_attn_kernel(q_ref, bias_ref, k_ref, v_ref, o_ref, s_ref, acc_ref, m_ref, *, tq, tk):
    i = pl.program_id(2)
    gp = q_ref.shape[1]
    rows = q_ref.shape[4]
    reps = rows // tq
    t0 = pl.multiple_of(i * tq, tq)
    n_main = (t0 + tk - 1) // tk
    tiles_per_group = SEL_GROUP * SEL_BLOCK // tk

    def scores(gg, kt):
        g0 = pl.multiple_of((kt // tiles_per_group) * SEL_GROUP, SEL_GROUP)
        b = bias_ref[0, gg, 0, pl.ds(g0, SEL_GROUP), :]
        qa = jnp.concatenate([q_ref[0, gg, 0], jnp.concatenate([b] * reps, axis=1)], axis=0)
        k = k_ref[0, gg, pl.ds(pl.multiple_of(kt * tk, tk), tk), :]
        return _dot(k, qa)

    def update(gg, slot, kt):
        s = s_ref[gg, slot]
        m = m_ref[gg]
        m_new = jnp.maximum(m, jnp.max(s, axis=0, keepdims=True))
        p = jnp.exp2(s - m_new).astype(BF16)
        v = v_ref[0, gg, :, pl.ds(pl.multiple_of(kt * tk, tk), tk)]
        acc_ref[gg] = jnp.exp2(m - m_new) * acc_ref[gg] + _dot(v, p)
        m_ref[gg] = m_new

    kpos = t0 + lax.broadcasted_iota(jnp.int32, (tq, rows), 0)
    t_col = t0 + lax.broadcasted_iota(jnp.int32, (tq, rows), 1) % tq
    for gg in range(gp):
        s = _dot(k_ref[0, gg, pl.ds(t0, tq), 0:HEAD_DIM], q_ref[0, gg, 0])
        s = jnp.where(kpos <= t_col, s, NEG)
        m = jnp.max(s, axis=0, keepdims=True)
        m_ref[gg] = m
        acc_ref[gg] = _dot(v_ref[0, gg, :, pl.ds(t0, tq)], jnp.exp2(s - m).astype(BF16))
        s_ref[gg, 0] = scores(gg, 0)

    def body(jj, carry):
        k0 = 2 * jj
        for gg in range(gp):
            s_ref[gg, 1] = scores(gg, k0 + 1)
        for gg in range(gp):
            update(gg, 0, k0)
        for gg in range(gp):
            s_ref[gg, 0] = scores(gg, jnp.minimum(k0 + 2, n_main - 1))
        for gg in range(gp):
            update(gg, 1, k0 + 1)
        return carry

    lax.fori_loop(0, n_main // 2, body, 0)

    @pl.when(n_main % 2 == 1)
    def _():
        for gg in range(gp):
            update(gg, 0, n_main - 1)

    for gg in range(gp):
        o = acc_ref[gg, 0:HEAD_DIM, :] / acc_ref[gg, HEAD_DIM:HEAD_DIM + 1, :]
        for r in range(reps):
            c0 = (gg * reps + r) * HEAD_DIM
            o_ref[:, c0:c0 + HEAD_DIM] = o[:, r * tq:(r + 1) * tq].T.astype(o_ref.dtype)


def _sel_attn(qt, bias_t, ka, vat, tq, tk):
    B, G, nt, hd, rows = qt.shape
    S = ka.shape[2]
    nb = bias_t.shape[3]
    gp = GROUP_PAIR
    return pl.pallas_call(
        functools.partial(_sel_attn_kernel, tq=tq, tk=tk),
        grid=(B, G // gp, nt),
        in_specs=[
            pl.BlockSpec((1, gp, 1, hd, rows), lambda b, g, i: (b, g, i, 0, 0)),
            pl.BlockSpec((1, gp, 1, nb, tq), lambda b, g, i: (b, g, i, 0, 0)),
            pl.BlockSpec((1, gp, S, LANES), lambda b, g, i: (b, g, 0, 0)),
            pl.BlockSpec((1, gp, PV_ROWS, S), lambda b, g, i: (b, g, 0, 0)),
        ],
        out_specs=pl.BlockSpec((tq, gp * rows // tq * hd), lambda b, g, i: (b * nt + i, g)),
        out_shape=jax.ShapeDtypeStruct((B * nt * tq, G * rows // tq * hd), BF16),
        scratch_shapes=[pltpu.VMEM((gp, 2, tk, rows), F32),
                        pltpu.VMEM((gp, PV_ROWS, rows), F32),
                        pltpu.VMEM((gp, 1, rows), F32)],
        compiler_params=_cparams("parallel", "parallel", "arbitrary"),
        name="sel_attn",
    )(qt, bias_t, ka, vat)


def _win_attn_kernel(q_ref, k_ref, v_ref, o_ref, *, tq):
    for gg in range(q_ref.shape[1]):
        _win_attn_group(q_ref.at[0, gg, 0], k_ref.at[0, gg], v_ref.at[0, gg],
                        o_ref.at[:, gg * NSA_HPG * HEAD_DIM:(gg + 1) * NSA_HPG * HEAD_DIM], tq=tq)


def _win_attn_group(q_ref, k_ref, v_ref, o_ref, *, tq):
    i = pl.program_id(2)
    q = q_ref[...]
    rows = q.shape[0]
    span = WINDOW + tq
    t0 = i * tq
    start = pl.multiple_of(jnp.maximum(t0 - WINDOW, 0), tq)
    k = k_ref[pl.ds(start, span), :]
    v = v_ref[pl.ds(start, span), :]
    s = _dot_nt(q, k)
    t_col = t0 + lax.broadcasted_iota(jnp.int32, (rows, 1), 0) % tq
    kpos = start + lax.broadcasted_iota(jnp.int32, (1, span), 1)
    dlt = lax.bitcast_convert_type(t_col - kpos, jnp.uint32)
    s = jnp.where(dlt < jnp.uint32(WINDOW), s, NEG)
    m = jnp.max(s, axis=-1, keepdims=True)
    ov = _dot(jnp.exp2(s - m).astype(BF16), v)
    o = (ov[:, 0:HEAD_DIM] / ov[:, HEAD_DIM:HEAD_DIM + 1]).astype(o_ref.dtype)
    for r in range(rows // tq):
        o_ref[:, r * HEAD_DIM:(r + 1) * HEAD_DIM] = o[r * tq:(r + 1) * tq]


def _win_attn(q5, kw, vwa, tq):
    B, G, nt, rows, hd = q5.shape
    S = kw.shape[2]
    gp = GROUP_PAIR
    return pl.pallas_call(
        functools.partial(_win_attn_kernel, tq=tq),
        grid=(B, G // gp, nt),
        in_specs=[
            pl.BlockSpec((1, gp, 1, rows, hd), lambda b, g, i: (b, g, i, 0, 0)),
            pl.BlockSpec((1, gp, S, hd), lambda b, g, i: (b, g, 0, 0)),
            pl.BlockSpec((1, gp, S, LANES), lambda b, g, i: (b, g, 0, 0)),
        ],
        out_specs=pl.BlockSpec((tq, gp * rows // tq * hd), lambda b, g, i: (b * nt + i, g)),
        out_shape=jax.ShapeDtypeStruct((B * nt * tq, G * rows // tq * hd), BF16),
        compiler_params=_cparams("parallel", "parallel", "arbitrary"),
        name="win_attn",
    )(q5, kw, vwa)


def _nsa_out_kernel(x_ref, oc_ref, os_ref, ow_ref, gate_ref, ex_ref, sz_ref, w_ref, y_ref):
    g_hi, g_lo = _split_bf16(gate_ref[...])
    o = None
    for c, o_ref in enumerate((oc_ref, os_ref, ow_ref)):
        e = ex_ref[c]
        term = (_dot(g_hi, e) + _dot(g_lo, e)) * o_ref[...].astype(F32)
        o = term if o is None else o + term
    u = (o * sz_ref[...].astype(F32)).astype(BF16)
    y_ref[...] = x_ref[...] + _dot(u, w_ref[...])


def _nsa_out(x2, oc, os_, ow, gates, expand, sz, w_out, tm):
    n, d = x2.shape
    row = lambda i: (i, 0)
    return pl.pallas_call(
        _nsa_out_kernel,
        grid=(n // tm,),
        in_specs=[
            pl.BlockSpec((tm, d), row),
            pl.BlockSpec((tm, d), row),
            pl.BlockSpec((tm, d), row),
            pl.BlockSpec((tm, d), row),
            pl.BlockSpec((tm, LANES), row),
            pl.BlockSpec(expand.shape, lambda i: (0, 0, 0)),
            pl.BlockSpec((tm, d), row),
            pl.BlockSpec(w_out.shape, lambda i: (0, 0)),
        ],
        out_specs=pl.BlockSpec((tm, d), row),
        out_shape=jax.ShapeDtypeStruct((n, d), F32),
        compiler_params=_cparams("parallel"),
        name="nsa_out",
    )(x2, oc, os_, ow, gates, expand, sz, w_out)


def _hg_proj_kernel(x_ref, g_ref, lb_ref, wq_ref, wf_ref, wi_ref, wz_ref,
                    q_ref, k_ref, lf_ref, v_ref, sz_ref):
    x = x_ref[...]
    h = x * lax.rsqrt(jnp.mean(x * x, axis=-1, keepdims=True) + EPS) * g_ref[...]
    hb = h.astype(BF16)
    qr = _dot(hb, wq_ref[...])
    q_ref[...] = (qr * _sigmoid(qr)).astype(BF16)
    lb = lb_ref[...]
    f = lb + (1.0 - lb) * _sigmoid(_dot(hb, wf_ref[...]))
    k_ref[...] = (1.0 - f).astype(BF16)
    lf_ref[...] = jnp.log(f)
    v_ref[...] = _dot(hb, wi_ref[...]).astype(BF16)
    z = _dot(hb, wz_ref[...])
    sz_ref[...] = (z * _sigmoid(z)).astype(BF16)


def _hg_proj(x2, g, lb, wq, wf, wi, wz, tm):
    n, d = x2.shape
    row = lambda i: (i, 0)
    fix = lambda i: (0, 0)
    return pl.pallas_call(
        _hg_proj_kernel,
        grid=(n // tm,),
        in_specs=[pl.BlockSpec((tm, d), row), pl.BlockSpec((1, d), fix), pl.BlockSpec((1, d), fix)]
        + [pl.BlockSpec((d, d), fix)] * 4,
        out_specs=[pl.BlockSpec((tm, d), row)] * 5,
        out_shape=[
            jax.ShapeDtypeStruct((n, d), BF16),
            jax.ShapeDtypeStruct((n, d), BF16),
            jax.ShapeDtypeStruct((n, d), F32),
            jax.ShapeDtypeStruct((n, d), BF16),
            jax.ShapeDtypeStruct((n, d), BF16),
        ],
        compiler_params=_cparams("parallel"),
        name="hg_proj",
    )(x2, g, lb, wq, wf, wi, wz)


def _hg_masks(c):
    t = np.arange(c)
    masks = []
    h = 1
    while h < c:
        same = (t[:, None] // (2 * h)) == (t[None, :] // (2 * h))
        upper = ((t[:, None] // h) % 2) == 1
        lower = ((t[None, :] // h) % 2) == 0
        masks.append((same & upper & lower).astype(np.float32))
        h *= 2
    masks.append(np.eye(c, dtype=np.float32))
    return np.stack(masks, axis=0)


def _hg_scan_kernel(q_ref, k_ref, lf_ref, v_ref, gn_ref, mk_ref, o_ref, st_ref, qd_ref, kd_ref, *, c):
    @pl.when(pl.program_id(2) == 0)
    def _():
        st_ref[...] = jnp.zeros_like(st_ref)

    ct = q_ref.shape[1]
    n_lvl = mk_ref.shape[0] - 1
    lf = lf_ref[0]
    pos = lax.broadcasted_iota(jnp.int32, lf.shape, 0) % c
    b = lf
    d = 1
    while d < c:
        b = b + jnp.where(pos >= d, pltpu.roll(b, d, 0), 0.0)
        d *= 2
    q = q_ref[0].astype(F32)
    k = k_ref[0].astype(F32)
    start, end = b - lf, b
    h = 1
    for lv in range(n_lvl + 1):
        qd_ref[lv] = (q * jnp.exp(b - start)).astype(BF16)
        kd_ref[lv] = (k * jnp.exp(end - b)).astype(BF16)
        if h < c:
            odd = (pos // h) % 2 == 1
            start = jnp.where(odd, pltpu.roll(start, h, 0), start)
            end = jnp.where(odd, end, pltpu.roll(end, ct - h, 0))
            h *= 2
    chunk_decay = jnp.exp(end)

    st = st_ref[...]
    for ci in range(ct // c):
        sl = slice(ci * c, (ci + 1) * c)
        v = v_ref[0, sl, :]
        att = mk_ref[n_lvl] * _dot_nt(q_ref[0, sl, :], k_ref[0, sl, :])
        for lv in range(n_lvl):
            att = att + mk_ref[lv] * _dot_nt(qd_ref[lv, sl, :], kd_ref[lv, sl, :])
        o = _dot_nt(qd_ref[n_lvl, sl, :], st.astype(BF16)) + _dot(att.astype(BF16), v)
        st = st * chunk_decay[ci * c:ci * c + 1, :] + _dot_tn(v, kd_ref[n_lvl, sl, :])
        on = o * lax.rsqrt(jnp.mean(o * o, axis=-1, keepdims=True) + EPS)
        o_ref[0, sl, :] = (on * gn_ref[...]).astype(o_ref.dtype)
    st_ref[...] = st


def _hg_scan(q, k, lf, v, gn, masks, ct):
    B, S, d = q.shape
    nh = d // HG_HEAD_DIM
    n_parts = masks.shape[0]
    blk = pl.BlockSpec((1, ct, HG_HEAD_DIM), lambda b, h, i: (b, i, h))
    return pl.pallas_call(
        functools.partial(_hg_scan_kernel, c=HG_CHUNK),
        grid=(B, nh, S // ct),
        in_specs=[blk, blk, blk, blk,
                  pl.BlockSpec((1, HG_HEAD_DIM), lambda b, h, i: (0, h)),
                  pl.BlockSpec(masks.shape, lambda b, h, i: (0, 0, 0))],
        out_specs=blk,
        out_shape=jax.ShapeDtypeStruct((B, S, d), BF16),
        scratch_shapes=[pltpu.VMEM((HG_HEAD_DIM, HG_HEAD_DIM), F32),
                        pltpu.VMEM((n_parts, ct, HG_HEAD_DIM), BF16),
                        pltpu.VMEM((n_parts, ct, HG_HEAD_DIM), BF16)],
        compiler_params=_cparams("parallel", "parallel", "arbitrary"),
        name="hg_scan",
    )(q, k, lf, v, gn, masks)


def _hg_out_kernel(x_ref, o_ref, sz_ref, w_ref, g_ref, y_ref):
    u = (o_ref[...].astype(F32) * sz_ref[...].astype(F32)).astype(BF16)
    x = x_ref[...] + _dot(u, w_ref[...])
    y_ref[...] = x * lax.rsqrt(jnp.mean(x * x, axis=-1, keepdims=True) + EPS) * g_ref[...]


def _hg_out(x2, o, sz, w_out, g, tm):
    n, d = x2.shape
    row = lambda i: (i, 0)
    fix = lambda i: (0, 0)
    return pl.pallas_call(
        _hg_out_kernel,
        grid=(n // tm,),
        in_specs=[pl.BlockSpec((tm, d), row), pl.BlockSpec((tm, d), row), pl.BlockSpec((tm, d), row),
                  pl.BlockSpec((d, d), fix), pl.BlockSpec((1, d), fix)],
        out_specs=pl.BlockSpec((tm, d), row),
        out_shape=jax.ShapeDtypeStruct((n, d), F32),
        compiler_params=_cparams("parallel"),
        name="hg_out",
    )(x2, o, sz, w_out, g)


def _nsa_layer(x2, B, S, g, w_in, pe_k, pe_v, wk1, wk2, wv1, wv2, w_out):
    H, G, R, hd = NSA_HEADS, NSA_GROUPS, NSA_HPG, HEAD_DIM
    n = B * S
    tm = 512
    tq = 128
    qw, kvw = H * hd, G * hd
    off = np.cumsum([0, qw] + [kvw] * 6 + [3 * H, qw])
    wq = w_in[:, off[0]:off[1]].astype(BF16)
    wkv = w_in[:, off[1]:off[7]].astype(BF16)
    wg = jnp.pad(w_in[:, off[7]:off[8]], ((0, 0), (0, LANES - 3 * H))).astype(BF16)
    wz = w_in[:, off[8]:off[9]].astype(BF16)
    pos = jnp.arange(S)
    cos, sin = _rope_tables(pos, LANES)
    onehot = jax.nn.one_hot((pos // SEL_BLOCK) % SEL_GROUP, SEL_GROUP, dtype=BF16)
    ones_col = jnp.zeros((1, LANES - hd), BF16).at[0, 0].set(1.0)
    q, q5, kvc, ka, kw, vs, vwa, gates, sz = _nsa_proj(
        x2, g[None, :], wq, wkv, wg, wz, cos, sin, onehot, jnp.broadcast_to(ones_col, (tm, LANES - hd)),
        S, tm, tq)

    nc = S // CMP_STRIDE
    per = CMP_BLOCK // CMP_STRIDE
    assert per == 2
    def chunks(a):
        a = a.reshape(B, nc, CMP_STRIDE, G, hd).transpose(0, 3, 1, 2, 4)
        return a.reshape(B, G, nc, CMP_STRIDE * hd)
    cmp_end = jnp.arange(nc) * CMP_STRIDE + CMP_BLOCK - 1
    ccos, csin = _rope_tables(cmp_end, hd)
    swap = lambda w: jnp.concatenate([w[:, hd // 2:], w[:, :hd // 2]], axis=1)
    k_cmp = _compress(chunks(kvc[:, 0:kvw]), pe_k.reshape(per, CMP_STRIDE * hd), wk1.astype(BF16),
                      wk2.astype(BF16), swap(wk2).astype(BF16), ccos, csin)
    v_cmp = _compress(chunks(kvc[:, kvw:2 * kvw]), pe_v.reshape(per, CMP_STRIDE * hd), wv1.astype(BF16),
                      wv2.astype(BF16), swap(wv2).astype(BF16),
                      jnp.ones_like(ccos), jnp.zeros_like(csin))

    nt = S // tq

    n_sel = S // SEL_BLOCK
    n_top = min(SEL_TOPK, n_sel)
    cs = np.arange(nc)[:, None] * CMP_STRIDE
    ss = np.arange(n_sel)[None, :] * SEL_BLOCK
    overlap = ((cs < ss + SEL_BLOCK) & (cs + CMP_BLOCK > ss)).astype(np.float32)
    ovt = jnp.asarray(overlap.T, dtype=BF16)
    with_ones = lambda v: jnp.concatenate(
        [v, jnp.broadcast_to(ones_col, v.shape[:-1] + (LANES - hd,))], axis=-1)
    o_c, bias_t = _cmp_select(q5, k_cmp, with_ones(v_cmp), ovt, tq, n_top)

    nsg = -(-n_sel // SEL_GROUP)
    bias_t = jnp.pad(bias_t, ((0, 0),) * 3 + ((0, nsg * SEL_GROUP - n_sel), (0, 0)), constant_values=NEG)
    qt = q.reshape(B, nt, tq, G, R, hd).transpose(0, 3, 1, 5, 4, 2).reshape(B, G, nt, hd, R * tq)
    ones_row = jnp.zeros((PV_ROWS - hd, S), BF16).at[0, :].set(1.0)
    vat = jnp.concatenate([jnp.swapaxes(vs, 2, 3), jnp.broadcast_to(ones_row, (B, G, PV_ROWS - hd, S))], axis=2)
    tk = min(512, S)
    o_s = _sel_attn(qt, bias_t, ka, vat, tq, tk)
    o_w = _win_attn(q5, kw, vwa, tq)

    col = np.arange(3 * H)
    expand = np.zeros((3, LANES, qw), np.float32)
    for c in range(3):
        head = col[col % 3 == c] // 3
        for hh, cc in zip(head, col[col % 3 == c]):
            expand[c, cc, hh * hd:(hh + 1) * hd] = 1.0
    return _nsa_out(x2, o_c, o_s, o_w, gates,
                    jnp.asarray(expand, dtype=BF16), sz, w_out.astype(BF16), tm)


def _hgrn_layer(x2, B, S, g, w_in, lb, g_norm, w_out, final_g):
    n, d = x2.shape
    tm = 512
    wq, wf, wi, wz = (w_in[:, i * d:(i + 1) * d].astype(BF16) for i in range(4))
    q, k, lf, v, sz = _hg_proj(x2, g[None, :], lb[None, :], wq, wf, wi, wz, tm)
    r3 = lambda a: a.reshape(B, S, d)
    o = _hg_scan(r3(q), r3(k), r3(lf), r3(v), g_norm[None, :], jnp.asarray(_hg_masks(HG_CHUNK)),
                 min(512, S))
    return _hg_out(x2, o.reshape(n, d), sz, w_out.astype(BF16), final_g[None, :], tm)


def kernel(x, norm_w, nsa_w_in, nsa_pe_k, nsa_pe_v, nsa_wk1, nsa_wk2, nsa_wv1, nsa_wv2,
           nsa_w_out, hg_w_in, hg_lb_logits, hg_norm, hg_w_out, final_norm):
    B, S, d = x.shape
    depth = norm_w.shape[0]
    assert depth == 2 and nsa_w_in.shape[0] == 1 and hg_w_in.shape[0] == 1
    p = jax.nn.softmax(hg_lb_logits.astype(F32), axis=0)
    lower_bounds = jnp.cumsum(p, axis=0) - p[0]
    x2 = x.reshape(B * S, d)
    x2 = _nsa_layer(x2, B, S, norm_w[0], nsa_w_in[0], nsa_pe_k[0], nsa_pe_v[0], nsa_wk1[0],
                    nsa_wk2[0], nsa_wv1[0], nsa_wv2[0], nsa_w_out[0])
    out = _hgrn_layer(x2, B, S, norm_w[1], hg_w_in[0], lower_bounds[1], hg_norm[0], hg_w_out[0],
                      final_norm)
    return out.reshape(B, S, d)
```
